```python
import math
import jax, jax.numpy as jnp
from jax import lax
import numpy as np

D_MODEL = 1024
BATCH = 16
SEQ = 4096
DEPTH = 2

D_MIX = 2 * D_MODEL
D_SSD = D_MIX // 2
SSD_HEAD_DIM = 64
SSD_HEADS = D_SSD // SSD_HEAD_DIM
SSD_GROUPS = 4
SSD_STATE = 128
CONV_WIDTH = 4
CHUNK = 128
CONV_CH = D_SSD + 2 * SSD_GROUPS * SSD_STATE
D_FOX = D_MIX - D_SSD
FOX_HEAD_DIM = 64
FOX_HEADS = D_FOX // FOX_HEAD_DIM
Q_BLOCK = 128
PROJ_SIZES = (D_SSD, CONV_CH, SSD_HEADS, D_FOX, D_FOX, D_FOX, FOX_HEADS)
D_PROJ = sum(PROJ_SIZES)
PROJ_SPLITS = tuple(int(s) for s in np.cumsum(PROJ_SIZES)[:-1])
D_FF = 2816
N_EXPERTS = 8
TOP_K = 2
N_DENSE = (DEPTH + 1) // 2
N_MOE = DEPTH // 2
EPS = 1e-5

kernel_name = 'hymba_ssd_fox_moe_trunk'


def rms_norm(x, g):
    xf = x.astype(jnp.float32)
    y = xf * lax.rsqrt(jnp.mean(xf * xf, axis=-1, keepdims=True) + EPS)
    return y.astype(x.dtype) * g


def swiglu(h, w_gate, w_up, w_down):
    return (jax.nn.silu(h @ w_gate) * (h @ w_up)) @ w_down


def causal_depthwise_conv(u, w, b):
    out = lax.conv_general_dilated(
        u, w[:, None, :], window_strides=(1,), padding=[(CONV_WIDTH - 1, 0)],
        dimension_numbers=('NWC', 'WIO', 'NWC'), feature_group_count=u.shape[-1])
    return out + b


def ssd_chunked(x, dt, a, b_mat, c_mat):
    bsz, seq = x.shape[0], x.shape[1]
    nc = seq // CHUNK
    r = SSD_HEADS // SSD_GROUPS
    xc = (x * dt[..., None]).reshape(bsz, nc, CHUNK, SSD_GROUPS, r, SSD_HEAD_DIM)
    a_dt = (dt * a).reshape(bsz, nc, CHUNK, SSD_GROUPS, r)
    bc = b_mat.reshape(bsz, nc, CHUNK, SSD_GROUPS, SSD_STATE)
    cc = c_mat.reshape(bsz, nc, CHUNK, SSD_GROUPS, SSD_STATE)
    a_cs = jnp.cumsum(a_dt, axis=2)
    causal = jnp.tril(jnp.ones((CHUNK, CHUNK), dtype=bool))[:, :, None, None]
    seg = a_cs[:, :, :, None] - a_cs[:, :, None, :]
    decay_in = jnp.where(causal, jnp.exp(jnp.where(causal, seg, 0.0)), 0.0)
    cb = jnp.einsum('bclgn,bcsgn->bclsg', cc, bc)
    y_diag = jnp.einsum('bclsgr,bcsgrp->bclgrp', cb[..., None] * decay_in, xc)
    decay_out = jnp.exp(a_cs[:, :, -1:] - a_cs)
    chunk_states = jnp.einsum('bclgn,bclgrp->bcgrpn', bc, xc * decay_out[..., None])
    chunk_decay = jnp.exp(a_cs[:, :, -1])

    def step(h, inp):
        st, dec = inp
        return h * dec[..., None, None] + st, h

    init = jnp.zeros((bsz, SSD_GROUPS, r, SSD_HEAD_DIM, SSD_STATE), chunk_states.dtype)
    _, prev = lax.scan(step, init, (jnp.moveaxis(chunk_states, 1, 0), jnp.moveaxis(chunk_decay, 1, 0)))
    prev = jnp.moveaxis(prev, 0, 1)
    y_off = jnp.einsum('bclgn,bcgrpn->bclgrp', cc, prev) * jnp.exp(a_cs)[..., None]
    return (y_diag + y_off).reshape(bsz, seq, SSD_HEADS, SSD_HEAD_DIM)


def forgetting_attention(q, k, v, log_f):
    bsz, seq = q.shape[0], q.shape[1]
    nb = seq // Q_BLOCK
    scale = FOX_HEAD_DIM ** -0.5
    cum_f = jnp.cumsum(log_f, axis=1)
    f_k = jnp.transpose(cum_f, (0, 2, 1))
    q_blocks = jnp.moveaxis(q.reshape(bsz, nb, Q_BLOCK, FOX_HEADS, FOX_HEAD_DIM), 1, 0)
    f_blocks = jnp.moveaxis(cum_f.reshape(bsz, nb, Q_BLOCK, FOX_HEADS), 1, 0)
    pos_blocks = jnp.arange(seq, dtype=jnp.int32).reshape(nb, Q_BLOCK)
    k_pos = jnp.arange(seq, dtype=jnp.int32)

    def block(args):
        qb, fq, qpos = args
        s = jnp.einsum('bqhd,bkhd->bhqk', qb, k).astype(jnp.float32) * scale
        s = s + jnp.transpose(fq, (0, 2, 1))[..., None] - f_k[:, :, None, :]
        s = jnp.where(k_pos[None, :] <= qpos[:, None], s, -jnp.inf)
        p = jax.nn.softmax(s, axis=-1).astype(v.dtype)
        return jnp.einsum('bhqk,bkhd->bqhd', p, v)

    out = lax.map(block, (q_blocks, f_blocks, pos_blocks))
    return jnp.moveaxis(out, 0, 1).reshape(bsz, seq, D_FOX)


def hybrid_mixer(h, w_in, conv_w, conv_b, dt_bias, a_log, d_skip, ssd_norm, fox_f_bias, fox_norm, w_out):
    bsz, seq, _ = h.shape
    proj = h @ w_in
    z, xbc, dt_raw, q, k, v, f_raw = jnp.split(proj, PROJ_SPLITS, axis=-1)
    xbc = jax.nn.silu(causal_depthwise_conv(xbc, conv_w, conv_b))
    xs, b_mat, c_mat = jnp.split(xbc, (D_SSD, D_SSD + SSD_GROUPS * SSD_STATE), axis=-1)
    xs = xs.reshape(bsz, seq, SSD_HEADS, SSD_HEAD_DIM)
    b_mat = b_mat.reshape(bsz, seq, SSD_GROUPS, SSD_STATE)
    c_mat = c_mat.reshape(bsz, seq, SSD_GROUPS, SSD_STATE)
    dt = jax.nn.softplus(dt_raw.astype(jnp.float32) + dt_bias.astype(jnp.float32))
    a = -jnp.exp(a_log.astype(jnp.float32))
    y = ssd_chunked(xs, dt, a, b_mat, c_mat) + d_skip[:, None] * xs
    y = y.reshape(bsz, seq, D_SSD).astype(h.dtype)
    y_ssd = rms_norm(y * jax.nn.silu(z), ssd_norm)
    log_f = jax.nn.log_sigmoid(f_raw.astype(jnp.float32) + fox_f_bias.astype(jnp.float32))
    y_fox = forgetting_attention(
        q.reshape(bsz, seq, FOX_HEADS, FOX_HEAD_DIM),
        k.reshape(bsz, seq, FOX_HEADS, FOX_HEAD_DIM),
        v.reshape(bsz, seq, FOX_HEADS, FOX_HEAD_DIM), log_f)
    y_fox = rms_norm(y_fox.astype(h.dtype), fox_norm)
    return jnp.concatenate([y_ssd, y_fox], axis=-1) @ w_out


def moe_swiglu(h, router_w, w_gate, w_up, w_down):
    bsz, seq, d = h.shape
    t = h.reshape(-1, d)
    logits = (t @ router_w).astype(jnp.float32)
    top_val, top_idx = lax.top_k(logits, TOP_K)
    top_w = jax.nn.softmax(top_val, axis=-1)
    combine = jnp.sum(jax.nn.one_hot(top_idx, N_EXPERTS, dtype=jnp.float32) * top_w[..., None], axis=1)
    combine = combine.astype(h.dtype)
    out = jnp.zeros_like(t)
    for e in range(N_EXPERTS):
        out = out + combine[:, e:e + 1] * swiglu(t, w_gate[e], w_up[e], w_down[e])
    return out.reshape(bsz, seq, d)


def setup_inputs(seed: int = 0) -> dict:
    key = jax.random.key(seed)
    ks = jax.random.split(key, 24)
    f32 = jnp.float32
    nrm = lambda k, shape, s: jax.random.normal(k, shape, f32) * s
    gain = lambda k, shape: 1.0 + 0.02 * jax.random.normal(k, shape, f32)
    u = jax.random.uniform(ks[5], (DEPTH, SSD_HEADS), f32)
    dt0 = jnp.exp(u * (math.log(0.1) - math.log(0.001)) + math.log(0.001))
    return {
        'x': jax.random.normal(ks[0], (BATCH, SEQ, D_MODEL), f32),
        'mix_norm': gain(ks[1], (DEPTH, D_MODEL)),
        'w_in': nrm(ks[2], (DEPTH, D_MODEL, D_PROJ), D_MODEL ** -0.5),
        'conv_w': nrm(ks[3], (DEPTH, CONV_WIDTH, CONV_CH), CONV_WIDTH ** -0.5),
        'conv_b': nrm(ks[4], (DEPTH, CONV_CH), 0.02),
        'dt_bias': dt0 + jnp.log(-jnp.expm1(-dt0)),
        'a_log': jnp.log(jax.random.uniform(ks[6], (DEPTH, SSD_HEADS), f32, 1.0, 16.0)),
        'd_skip': gain(ks[7], (DEPTH, SSD_HEADS)),
        'ssd_norm': gain(ks[8], (DEPTH, D_SSD)),
        'fox_f_bias': jax.random.uniform(ks[9], (DEPTH, FOX_HEADS), f32, 1.0, 4.0),
        'fox_norm': gain(ks[10], (DEPTH, D_FOX)),
        'w_out': nrm(ks[11], (DEPTH, D_MIX, D_MODEL), D_MIX ** -0.5),
        'ffn_norm': gain(ks[12], (DEPTH, D_MODEL)),
        'ffn_w_gate': nrm(ks[13], (N_DENSE, D_MODEL, D_FF), D_MODEL ** -0.5),
        'ffn_w_up': nrm(ks[14], (N_DENSE, D_MODEL, D_FF), D_MODEL ** -0.5),
        'ffn_w_down': nrm(ks[15], (N_DENSE, D_FF, D_MODEL), D_FF ** -0.5),
        'router_w': nrm(ks[16], (N_MOE, D_MODEL, N_EXPERTS), D_MODEL ** -0.5),
        'moe_w_gate': nrm(ks[17], (N_MOE, N_EXPERTS, D_MODEL, D_FF), D_MODEL ** -0.5),
        'moe_w_up': nrm(ks[18], (N_MOE, N_EXPERTS, D_MODEL, D_FF), D_MODEL ** -0.5),
        'moe_w_down': nrm(ks[19], (N_MOE, N_EXPERTS, D_FF, D_MODEL), D_FF ** -0.5),
        'final_norm': gain(ks[20], (D_MODEL,)),
    }


def reference(x, mix_norm, w_in, conv_w, conv_b, dt_bias, a_log, d_skip, ssd_norm, fox_f_bias,
              fox_norm, w_out, ffn_norm, ffn_w_gate, ffn_w_up, ffn_w_down, router_w,
              moe_w_gate, moe_w_up, moe_w_down, final_norm):
    for i in range(DEPTH):
        x = x + hybrid_mixer(rms_norm(x, mix_norm[i]), w_in[i], conv_w[i], conv_b[i], dt_bias[i],
                             a_log[i], d_skip[i], ssd_norm[i], fox_f_bias[i], fox_norm[i], w_out[i])
        hn = rms_norm(x, ffn_norm[i])
        j = i // 2
        if i % 2 == 0:
            x = x + swiglu(hn, ffn_w_gate[j], ffn_w_up[j], ffn_w_down[j])
        else:
            x = x + moe_swiglu(hn, router_w[j], moe_w_gate[j], moe_w_up[j], moe_w_down[j])
    return rms_norm(x, final_norm)
```

```python
import functools
import math

import jax
import jax.numpy as jnp
from jax import lax
from jax.experimental import pallas as pl
from jax.experimental.pallas import tpu as pltpu

F32 = jnp.float32
BF16 = jnp.bfloat16

D_MODEL = 1024
D_SSD = 1024
SSD_HEAD_DIM = 64
SSD_HEADS = 16
SSD_GROUPS = 4
SSD_STATE = 128
CONV_WIDTH = 4
CHUNK = 128
CONV_CH = D_SSD + 2 * SSD_GROUPS * SSD_STATE
D_FOX = 1024
FOX_HEAD_DIM = 64
FOX_HEADS = 16
D_MIX = D_SSD + D_FOX
D_FF = 2816
N_EXPERTS = 8
EPS = 1e-5

LANES = 128
SUBLANES = 8
VMEM_LIMIT_BYTES = 56 * 1024 * 1024

D_MAIN = CONV_CH + D_SSD + 3 * D_FOX
COL_Z = CONV_CH
COL_Q = COL_Z + D_SSD
COL_K = COL_Q + D_FOX
COL_V = COL_K + D_FOX
N_SMALL = SSD_HEADS + FOX_HEADS


def _params(*sem):
    return pltpu.CompilerParams(dimension_semantics=sem, vmem_limit_bytes=VMEM_LIMIT_BYTES)


def _split3(v):
    hi = v.astype(BF16)
    r1 = v - hi.astype(F32)
    mid = r1.astype(BF16)
    lo = (r1 - mid.astype(F32)).astype(BF16)
    return hi, mid, lo


def _dot(a, b):
    return jnp.dot(a, b, preferred_element_type=F32)


def _dot_nt(a, b):
    return lax.dot_general(a, b, (((1,), (1,)), ((), ())), preferred_element_type=F32)


def _dot_sel_right(v, sel_bf16):
    hi, mid, lo = _split3(v)
    return _dot(hi, sel_bf16) + _dot(mid, sel_bf16) + _dot(lo, sel_bf16)


def _dot_sel_left(sel_bf16, v):
    hi, mid, lo = _split3(v)
    return _dot(sel_bf16, hi) + _dot(sel_bf16, mid) + _dot(sel_bf16, lo)


def _softplus(x):
    return jnp.maximum(x, 0.0) + jnp.log1p(jnp.exp(-jnp.abs(x)))


def _silu(x):
    return x * jax.nn.sigmoid(x)


def _rms(x, g):
    ms = jnp.mean(x * x, axis=-1, keepdims=True)
    return (x * lax.rsqrt(ms + EPS)) * g


def _inproj_kernel(x_ref, g_ref, w_ref, ws_ref, wst_ref, proj_ref, small_ref, smallt_ref, hn_scr):
    @pl.when(pl.program_id(1) == 0)
    def _():
        hb = _rms(x_ref[...], g_ref[...]).astype(BF16)
        hn_scr[...] = hb
        small_ref[...] = _dot(hb, ws_ref[...])
        smallt_ref[...] = _dot_nt(wst_ref[...], hb)

    proj_ref[...] = _dot(hn_scr[...], w_ref[...]).astype(BF16)


def _inproj(x, g, w_main, w_small, w_small_t, tm, tn):
    t = x.shape[0]
    return pl.pallas_call(
        _inproj_kernel,
        grid=(t // tm, D_MAIN // tn),
        in_specs=[
            pl.BlockSpec((tm, D_MODEL), lambda i, j: (i, 0)),
            pl.BlockSpec((1, D_MODEL), lambda i, j: (0, 0)),
            pl.BlockSpec((D_MODEL, tn), lambda i, j: (0, j)),
            pl.BlockSpec((D_MODEL, LANES), lambda i, j: (0, 0)),
            pl.BlockSpec((N_SMALL, D_MODEL), lambda i, j: (0, 0)),
        ],
        out_specs=[
            pl.BlockSpec((tm, tn), lambda i, j: (i, j)),
            pl.BlockSpec((tm, LANES), lambda i, j: (i, 0)),
            pl.BlockSpec((N_SMALL, tm), lambda i, j: (0, i)),
        ],
        out_shape=[
            jax.ShapeDtypeStruct((t, D_MAIN), BF16),
            jax.ShapeDtypeStruct((t, LANES), F32),
            jax.ShapeDtypeStruct((N_SMALL, t), F32),
        ],
        scratch_shapes=[pltpu.VMEM((tm, D_MODEL), BF16)],
        compiler_params=_params("parallel", "arbitrary"),
        name="inproj",
    )(x, g, w_main, w_small, w_small_t)


def _fcum_kernel(ft_ref, fb_ref, triu_ref, out_ref, carry):
    @pl.when(pl.program_id(1) == 0)
    def _():
        carry[...] = jnp.zeros_like(carry)

    tl = ft_ref.shape[1]
    x = ft_ref[...] + fb_ref[...]
    logf = -_softplus(-x)
    cum = _dot_sel_right(logf, triu_ref[...]) + carry[:, 0:1]
    out_ref[0] = cum
    carry[...] = jnp.broadcast_to(cum[:, tl - 1:tl], carry.shape)


def _fcum(small_t, f_bias, bsz, seq, tl):
    nl = seq // tl
    fb = jnp.broadcast_to(f_bias.astype(F32)[:, None], (FOX_HEADS, tl))
    r = lax.broadcasted_iota(jnp.int32, (tl, tl), 0)
    c = lax.broadcasted_iota(jnp.int32, (tl, tl), 1)
    triu = (r <= c).astype(BF16)
    return pl.pallas_call(
        _fcum_kernel,
        grid=(bsz, nl),
        in_specs=[
            pl.BlockSpec((FOX_HEADS, tl), lambda b, j: (1, b * nl + j)),
            pl.BlockSpec((FOX_HEADS, tl), lambda b, j: (0, 0)),
            pl.BlockSpec((tl, tl), lambda b, j: (0, 0)),
        ],
        out_specs=pl.BlockSpec((1, FOX_HEADS, tl), lambda b, j: (b, 0, j)),
        out_shape=jax.ShapeDtypeStruct((bsz, FOX_HEADS, seq), F32),
        scratch_shapes=[pltpu.VMEM((FOX_HEADS, LANES), F32)],
        compiler_params=_params("parallel", "arbitrary"),
        name="fcum",
    )(small_t, fb, triu)


def _ssd_kernel(xbc_ref, z_ref, dt_ref, dtt_ref, cw_ref, cb_ref, dtb_ref, alog_ref, dtbc_ref,
                alogc_ref, exp_ref, dsk_ref, gn_ref, tril_ref, triu_ref, out_ref, xpad, state):
    pad = SUBLANES

    @pl.when(pl.program_id(1) == 0)
    def _():
        xpad[0:pad, :] = jnp.zeros((pad, CONV_CH), F32)
        state[...] = jnp.zeros_like(state)

    xpad[pad:pad + CHUNK, :] = xbc_ref[...].astype(F32)
    conv = cb_ref[...]
    for k in range(CONV_WIDTH):
        off = pad - (CONV_WIDTH - 1) + k
        conv = conv + cw_ref[k:k + 1, :] * xpad[off:off + CHUNK, :]
    xpad[0:pad, :] = xpad[CHUNK:CHUNK + pad, :]
    u = _silu(conv)
    xs = u[:, :D_SSD]
    gs = SSD_GROUPS * SSD_STATE
    bm = u[:, D_SSD:D_SSD + gs]
    cm = u[:, D_SSD + gs:]

    expand = exp_ref[...]
    dt = _softplus(dt_ref[...] + dtb_ref[...])
    a = -jnp.exp(alog_ref[...])
    acs = _dot_sel_left(tril_ref[...], dt * a)
    dtx = _dot_sel_right(dt, expand)
    acsx = _dot_sel_right(acs, expand)
    acs_last_x = acsx[CHUNK - 1:CHUNK, :]
    dtt = _softplus(dtt_ref[...] + dtbc_ref[...])
    at = -jnp.exp(alogc_ref[...])
    acst = _dot_sel_right(dtt * at, triu_ref[...])

    xc = xs * dtx
    xc_b = xc.astype(BF16)
    xdec_b = (xc * jnp.exp(acs_last_x - acsx)).astype(BF16)
    prev_b = state[...].astype(BF16)

    row = lax.broadcasted_iota(jnp.int32, (CHUNK, CHUNK), 0)
    col = lax.broadcasted_iota(jnp.int32, (CHUNK, CHUNK), 1)
    causal = col <= row
    low_half = lax.broadcasted_iota(jnp.int32, (CHUNK, LANES), 1) < SSD_HEAD_DIM
    heads_per_group = SSD_HEADS // SSD_GROUPS
    gw = heads_per_group * SSD_HEAD_DIM

    y_diag = []
    y_off = []
    st_new = []
    for g in range(SSD_GROUPS):
        bg = bm[:, g * SSD_STATE:(g + 1) * SSD_STATE]
        cg = cm[:, g * SSD_STATE:(g + 1) * SSD_STATE].astype(BF16)
        cbg = _dot_nt(cg, bg.astype(BF16))
        yd = []
        for r in range(heads_per_group):
            h = g * heads_per_group + r
            seg = acs[:, h:h + 1] - acst[h:h + 1, :]
            dec = jnp.where(causal, jnp.exp(jnp.where(causal, seg, 0.0)), 0.0)
            m = (cbg * dec).astype(BF16)
            j = h // 2
            yd.append(_dot(m, xc_b[:, j * LANES:(j + 1) * LANES]))
        for r in range(0, heads_per_group, 2):
            y_diag.append(jnp.where(low_half, yd[r], yd[r + 1]))
        sl = slice(g * gw, (g + 1) * gw)
        st_new.append(_dot(bg.T.astype(BF16), xdec_b[:, sl]))
        y_off.append(_dot(cg, prev_b[:, sl]))
    y_diag = jnp.concatenate(y_diag, axis=1)
    y_off = jnp.concatenate(y_off, axis=1) * jnp.exp(acsx)
    state[...] = state[...] * jnp.exp(acs_last_x) + jnp.concatenate(st_new, axis=1)

    y = y_diag + y_off + dsk_ref[...] * xs
    yg = y * _silu(z_ref[...].astype(F32))
    out_ref[...] = _rms(yg, gn_ref[...]).astype(BF16)


def _ssd(proj, small, small_t, conv_w, conv_b, dt_bias, a_log, d_skip, ssd_norm, bsz, seq):
    t = bsz * seq
    nc = seq // CHUNK
    pad_row = lambda v: jnp.zeros((1, LANES), F32).at[0, :SSD_HEADS].set(v.astype(F32))
    col = lambda v: jnp.broadcast_to(v.astype(F32)[:, None], (SSD_HEADS, CHUNK))
    hh = lax.broadcasted_iota(jnp.int32, (LANES, D_SSD), 0)
    cc = lax.broadcasted_iota(jnp.int32, (LANES, D_SSD), 1)
    expand = (cc // SSD_HEAD_DIM == hh).astype(BF16)
    r = lax.broadcasted_iota(jnp.int32, (CHUNK, CHUNK), 0)
    c = lax.broadcasted_iota(jnp.int32, (CHUNK, CHUNK), 1)
    tril = (c <= r).astype(BF16)
    triu = (r <= c).astype(BF16)
    dsk = jnp.repeat(d_skip.astype(F32), SSD_HEAD_DIM)[None, :]
    const = lambda shape: pl.BlockSpec(shape, lambda b, j: (0, 0))
    return pl.pallas_call(
        _ssd_kernel,
        grid=(bsz, nc),
        in_specs=[
            pl.BlockSpec((CHUNK, CONV_CH), lambda b, j: (b * nc + j, 0)),
            pl.BlockSpec((CHUNK, D_SSD), lambda b, j: (b * nc + j, COL_Z // D_SSD)),
            pl.BlockSpec((CHUNK, LANES), lambda b, j: (b * nc + j, 0)),
            pl.BlockSpec((SSD_HEADS, CHUNK), lambda b, j: (0, b * nc + j)),
            const((CONV_WIDTH, CONV_CH)),
            const((1, CONV_CH)),
            const((1, LANES)),
            const((1, LANES)),
            const((SSD_HEADS, CHUNK)),
            const((SSD_HEADS, CHUNK)),
            const((LANES, D_SSD)),
            const((1, D_SSD)),
            const((1, D_SSD)),
            const((CHUNK, CHUNK)),
            const((CHUNK, CHUNK)),
        ],
        out_specs=pl.BlockSpec((CHUNK, D_SSD), lambda b, j: (b * nc + j, 0)),
        out_shape=jax.ShapeDtypeStruct((t, D_SSD), BF16),
        scratch_shapes=[
            pltpu.VMEM((CHUNK + SUBLANES, CONV_CH), F32),
            pltpu.VMEM((SSD_STATE, D_SSD), F32),
        ],
        compiler_params=_params("parallel", "arbitrary"),
        name="ssd",
    )(proj, proj, small, small_t, conv_w.astype(F32), conv_b.astype(F32)[None, :], pad_row(dt_bias),
      pad_row(a_log), col(dt_bias), col(a_log), expand, dsk, ssd_norm.astype(F32)[None, :], tril, triu)


def _attn_kernel(qi_tab, ki_tab, q_ref, k_ref, v_ref, f_ref, o_ref, q_scr, m_scr, l_scr, acc_scr):
    hp = pl.program_id(1)
    p = pl.program_id(2)
    qi = qi_tab[p]
    ki = ki_tab[p]
    tq = q_ref.shape[0]
    tk = k_ref.shape[0]
    low_half = lax.broadcasted_iota(jnp.int32, (tq, LANES), 1) < FOX_HEAD_DIM

    @pl.when(ki == 0)
    def _():
        q = q_ref[...].astype(F32) * (FOX_HEAD_DIM ** -0.5)
        q_scr[0] = jnp.where(low_half, q, 0.0).astype(BF16)
        q_scr[1] = jnp.where(low_half, 0.0, q).astype(BF16)
        m_scr[...] = jnp.full(m_scr.shape, -jnp.inf, F32)
        l_scr[...] = jnp.zeros_like(l_scr)
        acc_scr[...] = jnp.zeros_like(acc_scr)

    def step(masked):
        k = k_ref[...]
        v = v_ref[...]
        f_all = f_ref[0]
        head_row = lax.broadcasted_iota(jnp.int32, f_all.shape, 0)
        frow = [jnp.sum(jnp.where(head_row == 2 * hp + h, f_all, 0.0), axis=0, keepdims=True)
                for h in range(2)]
        if masked:
            row = lax.broadcasted_iota(jnp.int32, (tq, tk), 0)
            col = lax.broadcasted_iota(jnp.int32, (tq, tk), 1)
            keep = col <= row
        upd = []
        for h in range(2):
            s = _dot_nt(q_scr[h], k) - frow[h]
            if masked:
                s = jnp.where(keep, s, -jnp.inf)
            m_prev = m_scr[h]
            m_new = jnp.maximum(m_prev, jnp.max(s, axis=-1, keepdims=True))
            alpha = jnp.exp(m_prev - m_new)
            pr = jnp.exp(s - m_new[:, 0:1])
            l_scr[h] = alpha * l_scr[h] + jnp.sum(pr, axis=-1, keepdims=True)
            m_scr[h] = m_new
            upd.append(alpha * acc_scr[...] + _dot(pr.astype(BF16), v))
        acc_scr[...] = jnp.where(low_half, upd[0], upd[1])

    @pl.when(ki < qi)
    def _():
        step(False)

    @pl.when(ki == qi)
    def _():
        step(True)
        o_ref[...] = (acc_scr[...] / jnp.where(low_half, l_scr[0], l_scr[1])).astype(BF16)


def _attention(proj, f_t, bsz, seq, tq):
    t = bsz * seq
    nq = seq // tq
    pairs = [(q, k) for q in range(nq) for k in range(q + 1)]
    qi_tab = jnp.asarray([p[0] for p in pairs], jnp.int32)
    ki_tab = jnp.asarray([p[1] for p in pairs], jnp.int32)
    n_hp = FOX_HEADS // 2
    cq, ck, cv = COL_Q // LANES, COL_K // LANES, COL_V // LANES
    grid_spec = pltpu.PrefetchScalarGridSpec(
        num_scalar_prefetch=2,
        grid=(bsz, n_hp, len(pairs)),
        in_specs=[
            pl.BlockSpec((tq, LANES), lambda b, h, p, qt, kt: (b * nq + qt[p], cq + h)),
            pl.BlockSpec((tq, LANES), lambda b, h, p, qt, kt: (b * nq + kt[p], ck + h)),
            pl.BlockSpec((tq, LANES), lambda b, h, p, qt, kt: (b * nq + kt[p], cv + h)),
            pl.BlockSpec((1, FOX_HEADS, tq), lambda b, h, p, qt, kt: (b, 0, kt[p])),
        ],
        out_specs=pl.BlockSpec((tq, LANES), lambda b, h, p, qt, kt: (b * nq + qt[p], h)),
        scratch_shapes=[
            pltpu.VMEM((2, tq, LANES), BF16),
            pltpu.VMEM((2, tq, LANES), F32),
            pltpu.VMEM((2, tq, LANES), F32),
            pltpu.VMEM((tq, LANES), F32),
        ],
    )
    return pl.pallas_call(
        _attn_kernel,
        grid_spec=grid_spec,
        out_shape=jax.ShapeDtypeStruct((t, D_FOX), BF16),
        compiler_params=_params("parallel", "parallel", "arbitrary"),
        name="fox_attention",
    )(qi_tab, ki_tab, proj, proj, proj, f_t)


def _outproj_kernel(moe, *refs):
    if moe:
        (ys_ref, yf_ref, x_ref, fg_ref, w_ref, ng_ref, wr_ref, ltri_ref,
         xo_ref, hn_ref, route_ref, cnt_ref, carry) = refs
    else:
        ys_ref, yf_ref, x_ref, fg_ref, w_ref, ng_ref, xo_ref, hn_ref = refs
    yf = _rms(yf_ref[...].astype(F32), fg_ref[...]).astype(BF16)
    mix = _dot(ys_ref[...], w_ref[0:D_SSD, :]) + _dot(yf, w_ref[D_SSD:D_MIX, :])
    x = x_ref[...] + mix
    xo_ref[...] = x
    hn = _rms(x, ng_ref[...])
    if not moe:
        hn_ref[...] = hn.astype(BF16)
        return
    hn_ref[...] = hn

    @pl.when(pl.program_id(0) == 0)
    def _():
        carry[...] = jnp.zeros_like(carry)

    tm = x.shape[0]
    h3 = _split3(hn)
    w3 = _split3(wr_ref[...])
    logits = (_dot(h3[0], w3[0]) + _dot(h3[0], w3[1]) + _dot(h3[1], w3[0])
              + _dot(h3[0], w3[2]) + _dot(h3[1], w3[1]) + _dot(h3[2], w3[0]))
    lane = lax.broadcasted_iota(jnp.int32, (tm, LANES), 1)
    lg = jnp.where(lane < N_EXPERTS, logits, -jnp.inf)
    m1 = jnp.max(lg, axis=-1, keepdims=True)
    i1 = jnp.min(jnp.where(lg == m1, lane, LANES), axis=-1, keepdims=True)
    lg2 = jnp.where(lane == i1, -jnp.inf, lg)
    m2 = jnp.max(lg2, axis=-1, keepdims=True)
    i2 = jnp.min(jnp.where(lg2 == m2, lane, LANES), axis=-1, keepdims=True)
    e2 = jnp.exp(m2 - m1)
    w1 = 1.0 / (1.0 + e2)
    w2 = e2 / (1.0 + e2)
    hit1 = lane == i1
    hit2 = lane == i2
    onehot = jnp.where(hit1 | hit2, 1.0, 0.0)
    before = _dot(ltri_ref[...], onehot.astype(BF16)) + carry[0:1, :]
    rank1 = jnp.sum(jnp.where(hit1, before, 0.0), axis=-1, keepdims=True)
    rank2 = jnp.sum(jnp.where(hit2, before, 0.0), axis=-1, keepdims=True)
    total = carry[0:1, :] + jnp.sum(onehot, axis=0, keepdims=True)
    carry[...] = jnp.broadcast_to(total, carry.shape)
    cnt_ref[...] = jnp.broadcast_to(total, cnt_ref.shape)
    route = jnp.where(lane == 0, i1.astype(F32), 0.0)
    route = jnp.where(lane == 1, i2.astype(F32), route)
    route = jnp.where(lane == 2, w1, route)
    route = jnp.where(lane == 3, w2, route)
    route = jnp.where(lane == 4, rank1, route)
    route = jnp.where(lane == 5, rank2, route)
    route_ref[...] = route


def _outproj(y_ssd, y_fox, x, fox_norm, w_out, next_norm, tm, router_w=None):
    t = x.shape[0]
    moe = router_w is not None
    row = lambda i: (i, 0)
    const = lambda i: (0, 0)
    in_specs = [
        pl.BlockSpec((tm, D_SSD), row),
        pl.BlockSpec((tm, D_FOX), row),
        pl.BlockSpec((tm, D_MODEL), row),
        pl.BlockSpec((1, D_FOX), const),
        pl.BlockSpec((D_MIX, D_MODEL), const),
        pl.BlockSpec((1, D_MODEL), const),
    ]
    args = [y_ssd, y_fox, x, fox_norm.astype(F32)[None, :], w_out.astype(BF16),
            next_norm.astype(F32)[None, :]]
    out_specs = [pl.BlockSpec((tm, D_MODEL), row), pl.BlockSpec((tm, D_MODEL), row)]
    out_shape = [jax.ShapeDtypeStruct((t, D_MODEL), F32),
                 jax.ShapeDtypeStruct((t, D_MODEL), F32 if moe else BF16)]
    scratch = []
    if moe:
        wr = jnp.zeros((D_MODEL, LANES), F32).at[:, :N_EXPERTS].set(router_w.astype(F32))
        r = lax.broadcasted_iota(jnp.int32, (tm, tm), 0)
        c = lax.broadcasted_iota(jnp.int32, (tm, tm), 1)
        ltri = (c < r).astype(BF16)
        in_specs += [pl.BlockSpec((D_MODEL, LANES), const), pl.BlockSpec((tm, tm), const)]
        args += [wr, ltri]
        out_specs += [pl.BlockSpec((tm, LANES), row), pl.BlockSpec((SUBLANES, LANES), const)]
        out_shape += [jax.ShapeDtypeStruct((t, LANES), F32),
                      jax.ShapeDtypeStruct((SUBLANES, LANES), F32)]
        scratch = [pltpu.VMEM((SUBLANES, LANES), F32)]
    return pl.pallas_call(
        functools.partial(_outproj_kernel, moe),
        grid=(t // tm,),
        in_specs=in_specs,
        out_specs=out_specs,
        out_shape=out_shape,
        scratch_shapes=scratch,
        compiler_params=_params("arbitrary" if moe else "parallel"),
        name="outproj_moe" if moe else "outproj",
    )(*args)


def _swiglu_kernel(residual, n_chunks, te_ref, nu_ref, *refs):
    if residual:
        h_ref, x_ref, wg_ref, wu_ref, wd_ref, o_ref = refs
    else:
        h_ref, wg_ref, wu_ref, wd_ref, o_ref = refs
    i = pl.program_id(0)

    @pl.when(i < nu_ref[0])
    def _():
        h = h_ref[...].astype(BF16)
        fc = D_FF // n_chunks
        acc = x_ref[...] if residual else None
        for c in range(n_chunks):
            g = _dot(h, wg_ref[0, :, c * fc:(c + 1) * fc])
            u = _dot(h, wu_ref[0, :, c * fc:(c + 1) * fc])
            a = (_silu(g) * u).astype(BF16)
            d = _dot(a, wd_ref[0, c * fc:(c + 1) * fc, :])
            acc = d if acc is None else acc + d
        o_ref[...] = acc

    @pl.when(i >= nu_ref[0])
    def _():
        o_ref[...] = jnp.zeros_like(o_ref)


def _swiglu(rows, tile_expert, n_used, w_gate, w_up, w_down, tm, x=None):
    r = rows.shape[0]
    residual = x is not None
    row = lambda i, te, nu: (i, 0)
    wspec = lambda shape: pl.BlockSpec(shape, lambda i, te, nu: (te[i], 0, 0),
                                       pipeline_mode=pl.Buffered(1))
    in_specs = [pl.BlockSpec((tm, D_MODEL), row)]
    args = [rows]
    if residual:
        in_specs.append(pl.BlockSpec((tm, D_MODEL), row))
        args.append(x)
    in_specs += [wspec((1, D_MODEL, D_FF)), wspec((1, D_MODEL, D_FF)), wspec((1, D_FF, D_MODEL))]
    args += [w_gate, w_up, w_down]
    grid_spec = pltpu.PrefetchScalarGridSpec(
        num_scalar_prefetch=2,
        grid=(r // tm,),
        in_specs=in_specs,
        out_specs=pl.BlockSpec((tm, D_MODEL), row),
    )
    return pl.pallas_call(
        functools.partial(_swiglu_kernel, residual, 2),
        grid_spec=grid_spec,
        out_shape=jax.ShapeDtypeStruct((r, D_MODEL), F32),
        compiler_params=_params("arbitrary"),
        name="swiglu_dense" if residual else "swiglu_experts",
    )(tile_expert, n_used, *args)


def _dispatch_kernel(pos_ref, h_ref, buf_ref, o_ref, sem):
    del buf_ref
    tm = h_ref.shape[0]

    def row_copy(r, slot):
        dst = pos_ref[0, 0, 2 * r + slot]
        return pltpu.make_async_copy(h_ref.at[pl.ds(r, 1), :], o_ref.at[pl.ds(dst, 1), :], sem)

    def issue(r, carry):
        row_copy(r, 0).start()
        row_copy(r, 1).start()
        return carry

    def drain(r, carry):
        row_copy(r, 0).wait()
        row_copy(r, 1).wait()
        return carry

    lax.fori_loop(0, tm, issue, 0)
    lax.fori_loop(0, tm, drain, 0)


def _dispatch(hn, pos, r_pad, tm):
    t = hn.shape[0]
    pos3 = pos.reshape(t // tm, 1, 2 * tm)
    buf = jnp.zeros((r_pad, D_MODEL), F32)
    return pl.pallas_call(
        _dispatch_kernel,
        grid=(t // tm,),
        in_specs=[
            pl.BlockSpec((1, 1, 2 * tm), lambda i: (i, 0, 0), memory_space=pltpu.SMEM),
            pl.BlockSpec((tm, D_MODEL), lambda i: (i, 0)),
            pl.BlockSpec(memory_space=pl.ANY),
        ],
        out_specs=pl.BlockSpec(memory_space=pl.ANY),
        out_shape=jax.ShapeDtypeStruct((r_pad, D_MODEL), F32),
        scratch_shapes=[pltpu.SemaphoreType.DMA],
        input_output_aliases={2: 0},
        compiler_params=_params("arbitrary"),
        name="moe_dispatch",
    )(pos3, hn, buf)


def _combine_kernel(pos_ref, route_ref, x_ref, g_ref, y_ref, o_ref, buf, sem):
    tm = x_ref.shape[0]

    def row_copy(r, slot):
        src = pos_ref[0, 0, 2 * r + slot]
        return pltpu.make_async_copy(y_ref.at[pl.ds(src, 1), :], buf.at[slot, pl.ds(r, 1), :], sem)

    def issue(r, carry):
        row_copy(r, 0).start()
        row_copy(r, 1).start()
        return carry

    def drain(r, carry):
        row_copy(r, 0).wait()
        row_copy(r, 1).wait()
        return carry

    lax.fori_loop(0, tm, issue, 0)
    lax.fori_loop(0, tm, drain, 0)
    route = route_ref[...]
    x = x_ref[...] + route[:, 2:3] * buf[0] + route[:, 3:4] * buf[1]
    o_ref[...] = _rms(x, g_ref[...])


def _combine(pos, route, x, final_norm, y_sorted, tm):
    t = x.shape[0]
    pos3 = pos.reshape(t // tm, 1, 2 * tm)
    return pl.pallas_call(
        _combine_kernel,
        grid=(t // tm,),
        in_specs=[
            pl.BlockSpec((1, 1, 2 * tm), lambda i: (i, 0, 0), memory_space=pltpu.SMEM),
            pl.BlockSpec((tm, LANES), lambda i: (i, 0)),
            pl.BlockSpec((tm, D_MODEL), lambda i: (i, 0)),
            pl.BlockSpec((1, D_MODEL), lambda i: (0, 0)),
            pl.BlockSpec(memory_space=pl.ANY),
        ],
        out_specs=pl.BlockSpec((tm, D_MODEL), lambda i: (i, 0)),
        out_shape=jax.ShapeDtypeStruct((t, D_MODEL), F32),
        scratch_shapes=[pltpu.VMEM((2, tm, D_MODEL), F32), pltpu.SemaphoreType.DMA],
        compiler_params=_params("arbitrary"),
        name="moe_combine",
    )(pos3, route, x, final_norm.astype(F32)[None, :], y_sorted)


def _tile(n, want):
    t = min(n, want)
    assert n % t == 0
    return t


def _split_w_in(w):
    sizes = (D_SSD, CONV_CH, SSD_HEADS, D_FOX, D_FOX, D_FOX, FOX_HEADS)
    offs = [0]
    for s in sizes:
        offs.append(offs[-1] + s)
    z, xbc, dt, q, k, v, f = (w[:, offs[i]:offs[i + 1]] for i in range(7))
    w_main = jnp.concatenate([xbc, z, q, k, v], axis=1).astype(BF16)
    small = jnp.concatenate([dt, f], axis=1)
    w_small = jnp.zeros((D_MODEL, LANES), F32).at[:, :N_SMALL].set(small).astype(BF16)
    return w_main, w_small, small.T.astype(BF16)


def kernel(x, mix_norm, w_in, conv_w, conv_b, dt_bias, a_log, d_skip, ssd_norm, fox_f_bias, fox_norm,
           w_out, ffn_norm, ffn_w_gate, ffn_w_up, ffn_w_down, router_w, moe_w_gate, moe_w_up,
           moe_w_down, final_norm):
    bsz, seq, _ = x.shape
    t = bsz * seq
    depth = mix_norm.shape[0]
    assert depth == 2 and seq % CHUNK == 0
    x = x.reshape(t, D_MODEL).astype(F32)
    tm_proj = _tile(t, 1024)
    tm_out = _tile(t, 512)
    tm_ffn = _tile(t, 512)
    tm_moe = _tile(t, 512)
    tm_row = _tile(t, 256)
    tq = _tile(seq, 512)

    def mixer(i, x, next_norm, router=None):
        w_main, w_small, w_small_t = _split_w_in(w_in[i])
        proj, small, small_t = _inproj(x, mix_norm[i].astype(F32)[None, :], w_main, w_small, w_small_t,
                                       tm_proj, 2048)
        y_ssd = _ssd(proj, small, small_t, conv_w[i], conv_b[i], dt_bias[i], a_log[i], d_skip[i],
                     ssd_norm[i], bsz, seq)
        f_t = _fcum(small_t, fox_f_bias[i], bsz, seq, tq)
        y_fox = _attention(proj, f_t, bsz, seq, tq)
        return _outproj(y_ssd, y_fox, x, fox_norm[i], w_out[i], next_norm, tm_out, router)

    x, hn = mixer(0, x, ffn_norm[0])
    n_tiles = t // tm_ffn
    x = _swiglu(hn, jnp.zeros((n_tiles,), jnp.int32), jnp.full((1,), n_tiles, jnp.int32),
                ffn_w_gate.astype(BF16), ffn_w_up.astype(BF16), ffn_w_down.astype(BF16), tm_ffn, x=x)

    x, hn, route, counts = mixer(1, x, ffn_norm[1], router_w[0])
    cnt = counts[0, :N_EXPERTS].astype(jnp.int32)
    group = ((cnt + tm_moe - 1) // tm_moe) * tm_moe
    ends = jnp.cumsum(group)
    starts = ends - group
    idx = route[:, 0:2].astype(jnp.int32)
    rank = route[:, 4:6].astype(jnp.int32)
    pos = (jnp.take(starts, idx) + rank).astype(jnp.int32)
    n_moe_tiles = (2 * t) // tm_moe + N_EXPERTS
    r_pad = n_moe_tiles * tm_moe
    tile_start = jnp.arange(n_moe_tiles, dtype=jnp.int32) * tm_moe
    tile_expert = jnp.minimum(jnp.sum(tile_start[:, None] >= ends[None, :], axis=1), N_EXPERTS - 1)
    n_used = (ends[-1:] // tm_moe).astype(jnp.int32)
    rows = _dispatch(hn, pos, r_pad, tm_row)
    y_sorted = _swiglu(rows, tile_expert.astype(jnp.int32), n_used, moe_w_gate[0].astype(BF16),
                       moe_w_up[0].astype(BF16), moe_w_down[0].astype(BF16), tm_moe)
    out = _combine(pos, route, x, final_norm, y_sorted, tm_row)
    return out.reshape(bsz, seq, D_MODEL)
```

```python
import functools
import math

import jax
import jax.numpy as jnp
from jax import lax
from jax.experimental import pallas as pl
from jax.experimental.pallas import tpu as pltpu

F32 = jnp.float32
BF16 = jnp.bfloat16

D_MODEL = 1024
D_SSD = 1024
SSD_HEAD_DIM = 64
SSD_HEADS = 16
SSD_GROUPS = 4
SSD_STATE = 128
CONV_WIDTH = 4
CHUNK = 128
CONV_CH = D_SSD + 2 * SSD_GROUPS * SSD_STATE
D_FOX = 1024
FOX_HEAD_DIM = 64
FOX_HEADS = 16
D_MIX = D_SSD + D_FOX
D_FF = 2816
N_EXPERTS = 8
EPS = 1e-5

LANES = 128
SUBLANES = 8
VMEM_LIMIT_BYTES = 56 * 1024 * 1024

D_MAIN = CONV_CH + D_SSD + 3 * D_FOX
COL_Z = CONV_CH
COL_Q = COL_Z + D_SSD
COL_K = COL_Q + D_FOX
COL_V = COL_K + D_FOX
N_SMALL = SSD_HEADS + FOX_HEADS


def _params(*sem):
    return pltpu.CompilerParams(dimension_semantics=sem, vmem_limit_bytes=VMEM_LIMIT_BYTES)


def _split3(v):
    hi = v.astype(BF16)
    r1 = v - hi.astype(F32)
    mid = r1.astype(BF16)
    lo = (r1 - mid.astype(F32)).astype(BF16)
    return hi, mid, lo


def _dot(a, b):
    return jnp.dot(a, b, preferred_element_type=F32)


def _dot_nt(a, b):
    return lax.dot_general(a, b, (((1,), (1,)), ((), ())), preferred_element_type=F32)


def _dot_sel_right(v, sel_bf16):
    hi, mid, lo = _split3(v)
    return _dot(hi, sel_bf16) + _dot(mid, sel_bf16) + _dot(lo, sel_bf16)


def _dot_sel_left(sel_bf16, v):
    hi, mid, lo = _split3(v)
    return _dot(sel_bf16, hi) + _dot(sel_bf16, mid) + _dot(sel_bf16, lo)


def _softplus(x):
    return jnp.maximum(x, 0.0) + jnp.log1p(jnp.exp(-jnp.abs(x)))


def _silu(x):
    return x * jax.nn.sigmoid(x)


def _rms(x, g):
    ms = jnp.mean(x * x, axis=-1, keepdims=True)
    return (x * lax.rsqrt(ms + EPS)) * g


def _inproj_kernel(x_ref, g_ref, w_ref, ws_ref, wst_ref, proj_ref, small_ref, smallt_ref, hn_scr):
    @pl.when(pl.program_id(1) == 0)
    def _():
        hb = _rms(x_ref[...], g_ref[...]).astype(BF16)
        hn_scr[...] = hb
        small_ref[...] = _dot(hb, ws_ref[...])
        smallt_ref[...] = _dot_nt(wst_ref[...], hb)

    proj_ref[...] = _dot(hn_scr[...], w_ref[...]).astype(BF16)


def _inproj(x, g, w_main, w_small, w_small_t, tm, tn):
    t = x.shape[0]
    return pl.pallas_call(
        _inproj_kernel,
        grid=(t // tm, D_MAIN // tn),
        in_specs=[
            pl.BlockSpec((tm, D_MODEL), lambda i, j: (i, 0)),
            pl.BlockSpec((1, D_MODEL), lambda i, j: (0, 0)),
            pl.BlockSpec((D_MODEL, tn), lambda i, j: (0, j)),
            pl.BlockSpec((D_MODEL, LANES), lambda i, j: (0, 0)),
            pl.BlockSpec((N_SMALL, D_MODEL), lambda i, j: (0, 0)),
        ],
        out_specs=[
            pl.BlockSpec((tm, tn), lambda i, j: (i, j)),
            pl.BlockSpec((tm, LANES), lambda i, j: (i, 0)),
            pl.BlockSpec((N_SMALL, tm), lambda i, j: (0, i)),
        ],
        out_shape=[
            jax.ShapeDtypeStruct((t, D_MAIN), BF16),
            jax.ShapeDtypeStruct((t, LANES), F32),
            jax.ShapeDtypeStruct((N_SMALL, t), F32),
        ],
        scratch_shapes=[pltpu.VMEM((tm, D_MODEL), BF16)],
        compiler_params=_params("parallel", "arbitrary"),
        name="inproj",
    )(x, g, w_main, w_small, w_small_t)


N_PIECES = 3


def _extra_lane(head):
    return (head // 2) * LANES + (FOX_HEAD_DIM if head % 2 == 0 else 0)


def _fprep_kernel(f_ref, fb_ref, tril_ref, place_ref, out_ref, carry):
    @pl.when(pl.program_id(1) == 0)
    def _():
        carry[...] = jnp.zeros_like(carry)

    tl = f_ref.shape[0]
    lane = lax.broadcasted_iota(jnp.int32, (tl, LANES), 1)
    is_f = (lane >= SSD_HEADS) & (lane < N_SMALL)
    logf = jnp.where(is_f, -_softplus(-(f_ref[...] + fb_ref[...])), 0.0)
    cum = _dot_sel_left(tril_ref[...], logf) + carry[0:1, :]
    carry[...] = jnp.broadcast_to(cum[tl - 1:tl, :], carry.shape)
    pieces = _split3(cum)
    out = _dot(pieces[0], place_ref[0])
    for i in range(1, N_PIECES):
        out = out + _dot(pieces[i], place_ref[i])
    out_ref[...] = out.astype(BF16)


def _fprep(small, f_bias, bsz, seq, tl):
    nl = seq // tl
    fb = jnp.zeros((1, LANES), F32).at[0, SSD_HEADS:N_SMALL].set(f_bias.astype(F32))
    r = lax.broadcasted_iota(jnp.int32, (tl, tl), 0)
    c = lax.broadcasted_iota(jnp.int32, (tl, tl), 1)
    tril = (c <= r).astype(BF16)
    heads = jnp.arange(FOX_HEADS)
    lanes = jnp.asarray([_extra_lane(h) for h in range(FOX_HEADS)])
    place = jnp.zeros((N_PIECES, LANES, D_FOX), F32)
    for i in range(N_PIECES):
        place = place.at[i, SSD_HEADS + heads, lanes + i].set(-1.0)
    return pl.pallas_call(
        _fprep_kernel,
        grid=(bsz, nl),
        in_specs=[
            pl.BlockSpec((tl, LANES), lambda b, j: (b * nl + j, 0)),
            pl.BlockSpec((1, LANES), lambda b, j: (0, 0)),
            pl.BlockSpec((tl, tl), lambda b, j: (0, 0)),
            pl.BlockSpec((N_PIECES, LANES, D_FOX), lambda b, j: (0, 0, 0)),
        ],
        out_specs=pl.BlockSpec((tl, D_FOX), lambda b, j: (b * nl + j, 0)),
        out_shape=jax.ShapeDtypeStruct((bsz * seq, D_FOX), BF16),
        scratch_shapes=[pltpu.VMEM((SUBLANES, LANES), F32)],
        compiler_params=_params("parallel", "arbitrary"),
        name="fprep",
    )(small, fb, tril, place.astype(BF16))


def _ssd_kernel(xbc_ref, z_ref, dt_ref, dtt_ref, cw_ref, cb_ref, dtb_ref, alog_ref, dtbc_ref,
                alogc_ref, exp_ref, dsk_ref, gn_ref, tril_ref, triu_ref, out_ref, xpad, state):
    pad = SUBLANES

    @pl.when(pl.program_id(1) == 0)
    def _():
        xpad[0:pad, :] = jnp.zeros((pad, CONV_CH), F32)
        state[...] = jnp.zeros_like(state)

    xpad[pad:pad + CHUNK, :] = xbc_ref[...].astype(F32)
    conv = cb_ref[...]
    for k in range(CONV_WIDTH):
        off = pad - (CONV_WIDTH - 1) + k
        conv = conv + cw_ref[k:k + 1, :] * xpad[off:off + CHUNK, :]
    xpad[0:pad, :] = xpad[CHUNK:CHUNK + pad, :]
    u = _silu(conv)
    xs = u[:, :D_SSD]
    gs = SSD_GROUPS * SSD_STATE
    bm = u[:, D_SSD:D_SSD + gs]
    cm = u[:, D_SSD + gs:]

    expand = exp_ref[...]
    dt = _softplus(dt_ref[...] + dtb_ref[...])
    a = -jnp.exp(alog_ref[...])
    acs = _dot_sel_left(tril_ref[...], dt * a)
    dtx = _dot_sel_right(dt, expand)
    acsx = _dot_sel_right(acs, expand)
    acs_last_x = acsx[CHUNK - 1:CHUNK, :]
    dtt = _softplus(dtt_ref[...] + dtbc_ref[...])
    at = -jnp.exp(alogc_ref[...])
    acst = _dot_sel_right(dtt * at, triu_ref[...])

    xc = xs * dtx
    xc_b = xc.astype(BF16)
    xdec_b = (xc * jnp.exp(acs_last_x - acsx)).astype(BF16)
    prev_b = state[...].astype(BF16)

    row = lax.broadcasted_iota(jnp.int32, (CHUNK, CHUNK), 0)
    col = lax.broadcasted_iota(jnp.int32, (CHUNK, CHUNK), 1)
    causal = col <= row
    low_half = lax.broadcasted_iota(jnp.int32, (CHUNK, LANES), 1) < SSD_HEAD_DIM
    heads_per_group = SSD_HEADS // SSD_GROUPS
    gw = heads_per_group * SSD_HEAD_DIM

    y_diag = []
    y_off = []
    st_new = []
    for g in range(SSD_GROUPS):
        bg = bm[:, g * SSD_STATE:(g + 1) * SSD_STATE]
        cg = cm[:, g * SSD_STATE:(g + 1) * SSD_STATE].astype(BF16)
        cbg = _dot_nt(cg, bg.astype(BF16))
        yd = []
        for r in range(heads_per_group):
            h = g * heads_per_group + r
            seg = acs[:, h:h + 1] - acst[h:h + 1, :]
            dec = jnp.where(causal, jnp.exp(jnp.where(causal, seg, 0.0)), 0.0)
            m = (cbg * dec).astype(BF16)
            j = h // 2
            yd.append(_dot(m, xc_b[:, j * LANES:(j + 1) * LANES]))
        for r in range(0, heads_per_group, 2):
            y_diag.append(jnp.where(low_half, yd[r], yd[r + 1]))
        sl = slice(g * gw, (g + 1) * gw)
        st_new.append(_dot(bg.T.astype(BF16), xdec_b[:, sl]))
        y_off.append(_dot(cg, prev_b[:, sl]))
    y_diag = jnp.concatenate(y_diag, axis=1)
    y_off = jnp.concatenate(y_off, axis=1) * jnp.exp(acsx)
    state[...] = state[...] * jnp.exp(acs_last_x) + jnp.concatenate(st_new, axis=1)

    y = y_diag + y_off + dsk_ref[...] * xs
    yg = y * _silu(z_ref[...].astype(F32))
    out_ref[...] = _rms(yg, gn_ref[...]).astype(BF16)


def _ssd(proj, small, small_t, conv_w, conv_b, dt_bias, a_log, d_skip, ssd_norm, bsz, seq):
    t = bsz * seq
    nc = seq // CHUNK
    pad_row = lambda v: jnp.zeros((1, LANES), F32).at[0, :SSD_HEADS].set(v.astype(F32))
    col = lambda v: jnp.broadcast_to(v.astype(F32)[:, None], (SSD_HEADS, CHUNK))
    hh = lax.broadcasted_iota(jnp.int32, (LANES, D_SSD), 0)
    cc = lax.broadcasted_iota(jnp.int32, (LANES, D_SSD), 1)
    expand = (cc // SSD_HEAD_DIM == hh).astype(BF16)
    r = lax.broadcasted_iota(jnp.int32, (CHUNK, CHUNK), 0)
    c = lax.broadcasted_iota(jnp.int32, (CHUNK, CHUNK), 1)
    tril = (c <= r).astype(BF16)
    triu = (r <= c).astype(BF16)
    dsk = jnp.repeat(d_skip.astype(F32), SSD_HEAD_DIM)[None, :]
    const = lambda shape: pl.BlockSpec(shape, lambda b, j: (0, 0))
    return pl.pallas_call(
        _ssd_kernel,
        grid=(bsz, nc),
        in_specs=[
            pl.BlockSpec((CHUNK, CONV_CH), lambda b, j: (b * nc + j, 0)),
            pl.BlockSpec((CHUNK, D_SSD), lambda b, j: (b * nc + j, COL_Z // D_SSD)),
            pl.BlockSpec((CHUNK, LANES), lambda b, j: (b * nc + j, 0)),
            pl.BlockSpec((SSD_HEADS, CHUNK), lambda b, j: (0, b * nc + j)),
            const((CONV_WIDTH, CONV_CH)),
            const((1, CONV_CH)),
            const((1, LANES)),
            const((1, LANES)),
            const((SSD_HEADS, CHUNK)),
            const((SSD_HEADS, CHUNK)),
            const((LANES, D_SSD)),
            const((1, D_SSD)),
            const((1, D_SSD)),
            const((CHUNK, CHUNK)),
            const((CHUNK, CHUNK)),
        ],
        out_specs=pl.BlockSpec((CHUNK, D_SSD), lambda b, j: (b * nc + j, 0)),
        out_shape=jax.ShapeDtypeStruct((t, D_SSD), BF16),
        scratch_shapes=[
            pltpu.VMEM((CHUNK + SUBLANES, CONV_CH), F32),
            pltpu.VMEM((SSD_STATE, D_SSD), F32),
        ],
        compiler_params=_params("parallel", "arbitrary"),
        name="ssd",
    )(proj, proj, small, small_t, conv_w.astype(F32), conv_b.astype(F32)[None, :], pad_row(dt_bias),
      pad_row(a_log), col(dt_bias), col(a_log), expand, dsk, ssd_norm.astype(F32)[None, :], tril, triu)


ATT_SUB = 512


def _attn_kernel(qi_tab, ki_tab, q_ref, k_ref, v_ref, fx_ref, o_ref, q_scr, m_scr, acc_scr):
    p = pl.program_id(2)
    qi = qi_tab[p]
    ki = ki_tab[p]
    tq = q_ref.shape[0]
    sub = min(ATT_SUB, tq)
    n_sub = tq // sub
    lane_q = lax.broadcasted_iota(jnp.int32, (tq, LANES), 1)
    lane = lax.broadcasted_iota(jnp.int32, (sub, LANES), 1)
    low = lane < FOX_HEAD_DIM
    own = (low, jnp.logical_not(low))
    extra0 = (FOX_HEAD_DIM, 0)

    @pl.when(ki == 0)
    def _():
        q = q_ref[...].astype(F32) * (FOX_HEAD_DIM ** -0.5)
        for h in range(2):
            is_one = (lane_q >= extra0[h]) & (lane_q < extra0[h] + N_PIECES)
            mine = (lane_q < FOX_HEAD_DIM) if h == 0 else (lane_q >= FOX_HEAD_DIM)
            q_scr[h] = jnp.where(mine, q, jnp.where(is_one, 1.0, 0.0)).astype(BF16)
        m_scr[...] = jnp.full(m_scr.shape, -jnp.inf, F32)
        acc_scr[...] = jnp.zeros_like(acc_scr)

    def sub_step(qs, ks, masked):
        rq = pl.ds(qs * sub, sub)
        rk = pl.ds(ks * sub, sub)
        k = k_ref[rk, :]
        v = v_ref[rk, :]
        fx = fx_ref[rk, :]
        if masked:
            row = lax.broadcasted_iota(jnp.int32, (sub, sub), 0)
            col = lax.broadcasted_iota(jnp.int32, (sub, sub), 1)
            keep = col <= row
        for h in range(2):
            k_aug = jnp.where(own[h], k, fx)
            v_aug = jnp.where(own[h], v, jnp.where(lane == extra0[h], 1.0, 0.0).astype(BF16))
            s = _dot_nt(q_scr[h, rq, :], k_aug)
            if masked:
                s = jnp.where(keep, s, -jnp.inf)
            tiles = [s[:, j * LANES:(j + 1) * LANES] for j in range(sub // LANES)]
            m_tile = functools.reduce(jnp.maximum, tiles)
            m_prev = m_scr[h, rq, :]
            m_new = jnp.maximum(m_prev, jnp.max(m_tile, axis=-1, keepdims=True))
            alpha = jnp.exp(m_prev - m_new)
            pr = jnp.concatenate([jnp.exp((t - m_new).astype(BF16)) for t in tiles], axis=1)
            m_scr[h, rq, :] = m_new
            acc_scr[h, rq, :] = alpha * acc_scr[h, rq, :] + _dot(pr, v_aug)

    @pl.when(ki < qi)
    def _():
        for qs in range(n_sub):
            for ks in range(n_sub):
                sub_step(qs, ks, False)

    @pl.when(ki == qi)
    def _():
        for qs in range(n_sub):
            for ks in range(qs + 1):
                sub_step(qs, ks, ks == qs)
        outs = []
        for h in range(2):
            acc = acc_scr[h]
            outs.append(acc / acc[:, extra0[h]:extra0[h] + 1])
        o_ref[...] = jnp.where(lane_q < FOX_HEAD_DIM, outs[0], outs[1]).astype(BF16)


def _attention(proj, fx, bsz, seq, tq):
    t = bsz * seq
    nq = seq // tq
    pairs = [(q, k) for q in range(nq) for k in range(q + 1)]
    qi_tab = jnp.asarray([p[0] for p in pairs], jnp.int32)
    ki_tab = jnp.asarray([p[1] for p in pairs], jnp.int32)
    n_hp = FOX_HEADS // 2
    cq, ck, cv = COL_Q // LANES, COL_K // LANES, COL_V // LANES
    grid_spec = pltpu.PrefetchScalarGridSpec(
        num_scalar_prefetch=2,
        grid=(bsz, n_hp, len(pairs)),
        in_specs=[
            pl.BlockSpec((tq, LANES), lambda b, h, p, qt, kt: (b * nq + qt[p], cq + h)),
            pl.BlockSpec((tq, LANES), lambda b, h, p, qt, kt: (b * nq + kt[p], ck + h)),
            pl.BlockSpec((tq, LANES), lambda b, h, p, qt, kt: (b * nq + kt[p], cv + h)),
            pl.BlockSpec((tq, LANES), lambda b, h, p, qt, kt: (b * nq + kt[p], h)),
        ],
        out_specs=pl.BlockSpec((tq, LANES), lambda b, h, p, qt, kt: (b * nq + qt[p], h)),
        scratch_shapes=[
            pltpu.VMEM((2, tq, LANES), BF16),
            pltpu.VMEM((2, tq, LANES), F32),
            pltpu.VMEM((2, tq, LANES), F32),
        ],
    )
    return pl.pallas_call(
        _attn_kernel,
        grid_spec=grid_spec,
        out_shape=jax.ShapeDtypeStruct((t, D_FOX), BF16),
        compiler_params=_params("parallel", "parallel", "arbitrary"),
        name="fox_attention",
    )(qi_tab, ki_tab, proj, proj, proj, fx)


def _outproj_kernel(moe, *refs):
    if moe:
        (ys_ref, yf_ref, x_ref, fg_ref, w_ref, ng_ref, wr_ref, ltri_ref,
         xo_ref, hn_ref, route_ref, cnt_ref, carry) = refs
    else:
        ys_ref, yf_ref, x_ref, fg_ref, w_ref, ng_ref, xo_ref, hn_ref = refs
    yf = _rms(yf_ref[...].astype(F32), fg_ref[...]).astype(BF16)
    mix = _dot(ys_ref[...], w_ref[0:D_SSD, :]) + _dot(yf, w_ref[D_SSD:D_MIX, :])
    x = x_ref[...] + mix
    xo_ref[...] = x
    hn = _rms(x, ng_ref[...])
    if not moe:
        hn_ref[...] = hn.astype(BF16)
        return
    hn_ref[...] = hn

    @pl.when(pl.program_id(0) == 0)
    def _():
        carry[...] = jnp.zeros_like(carry)

    tm = x.shape[0]
    h3 = _split3(hn)
    w3 = _split3(wr_ref[...])
    logits = (_dot(h3[0], w3[0]) + _dot(h3[0], w3[1]) + _dot(h3[1], w3[0])
              + _dot(h3[0], w3[2]) + _dot(h3[1], w3[1]) + _dot(h3[2], w3[0]))
    lane = lax.broadcasted_iota(jnp.int32, (tm, LANES), 1)
    lg = jnp.where(lane < N_EXPERTS, logits, -jnp.inf)
    m1 = jnp.max(lg, axis=-1, keepdims=True)
    i1 = jnp.min(jnp.where(lg == m1, lane, LANES), axis=-1, keepdims=True)
    lg2 = jnp.where(lane == i1, -jnp.inf, lg)
    m2 = jnp.max(lg2, axis=-1, keepdims=True)
    i2 = jnp.min(jnp.where(lg2 == m2, lane, LANES), axis=-1, keepdims=True)
    e2 = jnp.exp(m2 - m1)
    w1 = 1.0 / (1.0 + e2)
    w2 = e2 / (1.0 + e2)
    hit1 = lane == i1
    hit2 = lane == i2
    onehot = jnp.where(hit1 | hit2, 1.0, 0.0)
    before = _dot(ltri_ref[...], onehot.astype(BF16)) + carry[0:1, :]
    rank1 = jnp.sum(jnp.where(hit1, before, 0.0), axis=-1, keepdims=True)
    rank2 = jnp.sum(jnp.where(hit2, before, 0.0), axis=-1, keepdims=True)
    total = carry[0:1, :] + jnp.sum(onehot, axis=0, keepdims=True)
    carry[...] = jnp.broadcast_to(total, carry.shape)
    cnt_ref[...] = jnp.broadcast_to(total, cnt_ref.shape)
    route = jnp.where(lane == 0, i1.astype(F32), 0.0)
    route = jnp.where(lane == 1, i2.astype(F32), route)
    route = jnp.where(lane == 2, w1, route)
    route = jnp.where(lane == 3, w2, route)
    route = jnp.where(lane == 4, rank1, route)
    route = jnp.where(lane == 5, rank2, route)
    route_ref[...] = route


def _outproj(y_ssd, y_fox, x, fox_norm, w_out, next_norm, tm, router_w=None):
    t = x.shape[0]
    moe = router_w is not None
    row = lambda i: (i, 0)
    const = lambda i: (0, 0)
    in_specs = [
        pl.BlockSpec((tm, D_SSD), row),
        pl.BlockSpec((tm, D_FOX), row),
        pl.BlockSpec((tm, D_MODEL), row),
        pl.BlockSpec((1, D_FOX), const),
        pl.BlockSpec((D_MIX, D_MODEL), const),
        pl.BlockSpec((1, D_MODEL), const),
    ]
    args = [y_ssd, y_fox, x, fox_norm.astype(F32)[None, :], w_out.astype(BF16),
            next_norm.astype(F32)[None, :]]
    out_specs = [pl.BlockSpec((tm, D_MODEL), row), pl.BlockSpec((tm, D_MODEL), row)]
    out_shape = [jax.ShapeDtypeStruct((t, D_MODEL), F32),
                 jax.ShapeDtypeStruct((t, D_MODEL), F32 if moe else BF16)]
    scratch = []
    if moe:
        wr = jnp.zeros((D_MODEL, LANES), F32).at[:, :N_EXPERTS].set(router_w.astype(F32))
        r = lax.broadcasted_iota(jnp.int32, (tm, tm), 0)
        c = lax.broadcasted_iota(jnp.int32, (tm, tm), 1)
        ltri = (c < r).astype(BF16)
        in_specs += [pl.BlockSpec((D_MODEL, LANES), const), pl.BlockSpec((tm, tm), const)]
        args += [wr, ltri]
        out_specs += [pl.BlockSpec((tm, LANES), row), pl.BlockSpec((SUBLANES, LANES), const)]
        out_shape += [jax.ShapeDtypeStruct((t, LANES), F32),
                      jax.ShapeDtypeStruct((SUBLANES, LANES), F32)]
        scratch = [pltpu.VMEM((SUBLANES, LANES), F32)]
    return pl.pallas_call(
        functools.partial(_outproj_kernel, moe),
        grid=(t // tm,),
        in_specs=in_specs,
        out_specs=out_specs,
        out_shape=out_shape,
        scratch_shapes=scratch,
        compiler_params=_params("arbitrary" if moe else "parallel"),
        name="outproj_moe" if moe else "outproj",
    )(*args)


def _swiglu_kernel(residual, n_chunks, te_ref, nu_ref, *refs):
    if residual:
        h_ref, x_ref, wg_ref, wu_ref, wd_ref, o_ref = refs
    else:
        h_ref, wg_ref, wu_ref, wd_ref, o_ref = refs
    i = pl.program_id(0)

    @pl.when(i < nu_ref[0])
    def _():
        h = h_ref[...].astype(BF16)
        fc = D_FF // n_chunks
        acc = x_ref[...] if residual else None
        for c in range(n_chunks):
            g = _dot(h, wg_ref[0, :, c * fc:(c + 1) * fc])
            u = _dot(h, wu_ref[0, :, c * fc:(c + 1) * fc])
            a = (_silu(g) * u).astype(BF16)
            d = _dot(a, wd_ref[0, c * fc:(c + 1) * fc, :])
            acc = d if acc is None else acc + d
        o_ref[...] = acc

    @pl.when(i >= nu_ref[0])
    def _():
        o_ref[...] = jnp.zeros_like(o_ref)


def _swiglu(rows, tile_expert, n_used, w_gate, w_up, w_down, tm, x=None):
    r = rows.shape[0]
    residual = x is not None
    row = lambda i, te, nu: (i, 0)
    wspec = lambda shape: pl.BlockSpec(shape, lambda i, te, nu: (te[i], 0, 0),
                                       pipeline_mode=pl.Buffered(1))
    in_specs = [pl.BlockSpec((tm, D_MODEL), row)]
    args = [rows]
    if residual:
        in_specs.append(pl.BlockSpec((tm, D_MODEL), row))
        args.append(x)
    in_specs += [wspec((1, D_MODEL, D_FF)), wspec((1, D_MODEL, D_FF)), wspec((1, D_FF, D_MODEL))]
    args += [w_gate, w_up, w_down]
    grid_spec = pltpu.PrefetchScalarGridSpec(
        num_scalar_prefetch=2,
        grid=(r // tm,),
        in_specs=in_specs,
        out_specs=pl.BlockSpec((tm, D_MODEL), row),
    )
    return pl.pallas_call(
        functools.partial(_swiglu_kernel, residual, 2),
        grid_spec=grid_spec,
        out_shape=jax.ShapeDtypeStruct((r, D_MODEL), F32),
        compiler_params=_params("arbitrary"),
        name="swiglu_dense" if residual else "swiglu_experts",
    )(tile_expert, n_used, *args)


def _dispatch_kernel(pos_ref, h_ref, buf_ref, o_ref, sem):
    del buf_ref
    tm = h_ref.shape[0]

    def row_copy(r, slot):
        dst = pos_ref[0, 0, 2 * r + slot]
        return pltpu.make_async_copy(h_ref.at[pl.ds(r, 1), :], o_ref.at[pl.ds(dst, 1), :], sem)

    def issue(r, carry):
        row_copy(r, 0).start()
        row_copy(r, 1).start()
        return carry

    def drain(r, carry):
        row_copy(r, 0).wait()
        row_copy(r, 1).wait()
        return carry

    lax.fori_loop(0, tm, issue, 0)
    lax.fori_loop(0, tm, drain, 0)


def _dispatch(hn, pos, r_pad, tm):
    t = hn.shape[0]
    pos3 = pos.reshape(t // tm, 1, 2 * tm)
    buf = jnp.zeros((r_pad, D_MODEL), F32)
    return pl.pallas_call(
        _dispatch_kernel,
        grid=(t // tm,),
        in_specs=[
            pl.BlockSpec((1, 1, 2 * tm), lambda i: (i, 0, 0), memory_space=pltpu.SMEM),
            pl.BlockSpec((tm, D_MODEL), lambda i: (i, 0)),
            pl.BlockSpec(memory_space=pl.ANY),
        ],
        out_specs=pl.BlockSpec(memory_space=pl.ANY),
        out_shape=jax.ShapeDtypeStruct((r_pad, D_MODEL), F32),
        scratch_shapes=[pltpu.SemaphoreType.DMA],
        input_output_aliases={2: 0},
        compiler_params=_params("arbitrary"),
        name="moe_dispatch",
    )(pos3, hn, buf)


def _combine_kernel(pos_ref, route_ref, x_ref, g_ref, y_ref, o_ref, buf, sem):
    tm = x_ref.shape[0]

    def row_copy(r, slot):
        src = pos_ref[0, 0, 2 * r + slot]
        return pltpu.make_async_copy(y_ref.at[pl.ds(src, 1), :], buf.at[slot, pl.ds(r, 1), :], sem)

    def issue(r, carry):
        row_copy(r, 0).start()
        row_copy(r, 1).start()
        return carry

    def drain(r, carry):
        row_copy(r, 0).wait()
        row_copy(r, 1).wait()
        return carry

    lax.fori_loop(0, tm, issue, 0)
    lax.fori_loop(0, tm, drain, 0)
    route = route_ref[...]
    x = x_ref[...] + route[:, 2:3] * buf[0] + route[:, 3:4] * buf[1]
    o_ref[...] = _rms(x, g_ref[...])


def _combine(pos, route, x, final_norm, y_sorted, tm):
    t = x.shape[0]
    pos3 = pos.reshape(t // tm, 1, 2 * tm)
    return pl.pallas_call(
        _combine_kernel,
        grid=(t // tm,),
        in_specs=[
            pl.BlockSpec((1, 1, 2 * tm), lambda i: (i, 0, 0), memory_space=pltpu.SMEM),
            pl.BlockSpec((tm, LANES), lambda i: (i, 0)),
            pl.BlockSpec((tm, D_MODEL), lambda i: (i, 0)),
            pl.BlockSpec((1, D_MODEL), lambda i: (0, 0)),
            pl.BlockSpec(memory_space=pl.ANY),
        ],
        out_specs=pl.BlockSpec((tm, D_MODEL), lambda i: (i, 0)),
        out_shape=jax.ShapeDtypeStruct((t, D_MODEL), F32),
        scratch_shapes=[pltpu.VMEM((2, tm, D_MODEL), F32), pltpu.SemaphoreType.DMA],
        compiler_params=_params("arbitrary"),
        name="moe_combine",
    )(pos3, route, x, final_norm.astype(F32)[None, :], y_sorted)


def _tile(n, want):
    t = min(n, want)
    assert n % t == 0
    return t


def _split_w_in(w):
    sizes = (D_SSD, CONV_CH, SSD_HEADS, D_FOX, D_FOX, D_FOX, FOX_HEADS)
    offs = [0]
    for s in sizes:
        offs.append(offs[-1] + s)
    z, xbc, dt, q, k, v, f = (w[:, offs[i]:offs[i + 1]] for i in range(7))
    w_main = jnp.concatenate([xbc, z, q, k, v], axis=1).astype(BF16)
    small = jnp.concatenate([dt, f], axis=1)
    w_small = jnp.zeros((D_MODEL, LANES), F32).at[:, :N_SMALL].set(small).astype(BF16)
    return w_main, w_small, small.T.astype(BF16)


def kernel(x, mix_norm, w_in, conv_w, conv_b, dt_bias, a_log, d_skip, ssd_norm, fox_f_bias, fox_norm,
           w_out, ffn_norm, ffn_w_gate, ffn_w_up, ffn_w_down, router_w, moe_w_gate, moe_w_up,
           moe_w_down, final_norm):
    bsz, seq, _ = x.shape
    t = bsz * seq
    depth = mix_norm.shape[0]
    assert depth == 2 and seq % CHUNK == 0
    x = x.reshape(t, D_MODEL).astype(F32)
    tm_proj = _tile(t, 1024)
    tm_out = _tile(t, 512)
    tm_ffn = _tile(t, 512)
    tm_moe = _tile(t, 512)
    tm_row = _tile(t, 256)
    tq = _tile(seq, 1024)

    def mixer(i, x, next_norm, router=None):
        w_main, w_small, w_small_t = _split_w_in(w_in[i])
        proj, small, small_t = _inproj(x, mix_norm[i].astype(F32)[None, :], w_main, w_small, w_small_t,
                                       tm_proj, 2048)
        y_ssd = _ssd(proj, small, small_t, conv_w[i], conv_b[i], dt_bias[i], a_log[i], d_skip[i],
                     ssd_norm[i], bsz, seq)
        fx = _fprep(small, fox_f_bias[i], bsz, seq, _tile(seq, 512))
        y_fox = _attention(proj, fx, bsz, seq, tq)
        return _outproj(y_ssd, y_fox, x, fox_norm[i], w_out[i], next_norm, tm_out, router)

    x, hn = mixer(0, x, ffn_norm[0])
    n_tiles = t // tm_ffn
    x = _swiglu(hn, jnp.zeros((n_tiles,), jnp.int32), jnp.full((1,), n_tiles, jnp.int32),
                ffn_w_gate.astype(BF16), ffn_w_up.astype(BF16), ffn_w_down.astype(BF16), tm_ffn, x=x)

    x, hn, route, counts = mixer(1, x, ffn_norm[1], router_w[0])
    cnt = counts[0, :N_EXPERTS].astype(jnp.int32)
    group = ((cnt + tm_moe - 1) // tm_moe) * tm_moe
    ends = jnp.cumsum(group)
    starts = ends - group
    idx = route[:, 0:2].astype(jnp.int32)
    rank = route[:, 4:6].astype(jnp.int32)
    pos = (jnp.take(starts, idx) + rank).astype(jnp.int32)
    n_moe_tiles = (2 * t) // tm_moe + N_EXPERTS
    r_pad = n_moe_tiles * tm_moe
    tile_start = jnp.arange(n_moe_tiles, dtype=jnp.int32) * tm_moe
    tile_expert = jnp.minimum(jnp.sum(tile_start[:, None] >= ends[None, :], axis=1), N_EXPERTS - 1)
    n_used = (ends[-1:] // tm_moe).astype(jnp.int32)
    rows = _dispatch(hn, pos, r_pad, tm_row)
    y_sorted = _swiglu(rows, tile_expert.astype(jnp.int32), n_used, moe_w_gate[0].astype(BF16),
                       moe_w_up[0].astype(BF16), moe_w_down[0].astype(BF16), tm_moe)
    out = _combine(pos, route, x, final_norm, y_sorted, tm_row)
    return out.reshape(bsz, seq, D_MODEL)
```

```python
import functools
import math

import jax
import jax.numpy as jnp
from jax import lax
from jax.experimental import pallas as pl
from jax.experimental.pallas import tpu as pltpu

F32 = jnp.float32
BF16 = jnp.bfloat16

D_MODEL = 1024
D_SSD = 1024
SSD_HEAD_DIM = 64
SSD_HEADS = 16
SSD_GROUPS = 4
SSD_STATE = 128
CONV_WIDTH = 4
CHUNK = 128
CONV_CH = D_SSD + 2 * SSD_GROUPS * SSD_STATE
D_FOX = 1024
FOX_HEAD_DIM = 64
FOX_HEADS = 16
D_MIX = D_SSD + D_FOX
D_FF = 2816
N_EXPERTS = 8
EPS = 1e-5

LANES = 128
SUBLANES = 8
VMEM_LIMIT_BYTES = 56 * 1024 * 1024

D_MAIN = CONV_CH + D_SSD + 3 * D_FOX
COL_Z = CONV_CH
COL_Q = COL_Z + D_SSD
COL_K = COL_Q + D_FOX
COL_V = COL_K + D_FOX
N_SMALL = SSD_HEADS + FOX_HEADS


def _params(*sem):
    return pltpu.CompilerParams(dimension_semantics=sem, vmem_limit_bytes=VMEM_LIMIT_BYTES)


def _split3(v):
    hi = v.astype(BF16)
    r1 = v - hi.astype(F32)
    mid = r1.astype(BF16)
    lo = (r1 - mid.astype(F32)).astype(BF16)
    return hi, mid, lo


def _dot(a, b):
    return jnp.dot(a, b, preferred_element_type=F32)


def _dot_nt(a, b):
    return lax.dot_general(a, b, (((1,), (1,)), ((), ())), preferred_element_type=F32)


def _dot_sel_right(v, sel_bf16):
    hi, mid, lo = _split3(v)
    return _dot(hi, sel_bf16) + _dot(mid, sel_bf16) + _dot(lo, sel_bf16)


def _dot_sel_left(sel_bf16, v):
    hi, mid, lo = _split3(v)
    return _dot(sel_bf16, hi) + _dot(sel_bf16, mid) + _dot(sel_bf16, lo)


def _softplus(x):
    return jnp.maximum(x, 0.0) + jnp.log1p(jnp.exp(-jnp.abs(x)))


def _silu(x):
    return x * jax.nn.sigmoid(x)


def _rms(x, g):
    ms = jnp.mean(x * x, axis=-1, keepdims=True)
    return (x * lax.rsqrt(ms + EPS)) * g


def _inproj_kernel(x_ref, g_ref, w_ref, ws_ref, wst_ref, proj_ref, small_ref, smallt_ref, hn_scr):
    @pl.when(pl.program_id(1) == 0)
    def _():
        hb = _rms(x_ref[...], g_ref[...]).astype(BF16)
        hn_scr[...] = hb
        small_ref[...] = _dot(hb, ws_ref[...])
        smallt_ref[...] = _dot_nt(wst_ref[...], hb)

    proj_ref[...] = _dot(hn_scr[...], w_ref[...]).astype(BF16)


def _inproj(x, g, w_main, w_small, w_small_t, tm, tn):
    t = x.shape[0]
    return pl.pallas_call(
        _inproj_kernel,
        grid=(t // tm, D_MAIN // tn),
        in_specs=[
            pl.BlockSpec((tm, D_MODEL), lambda i, j: (i, 0)),
            pl.BlockSpec((1, D_MODEL), lambda i, j: (0, 0)),
            pl.BlockSpec((D_MODEL, tn), lambda i, j: (0, j)),
            pl.BlockSpec((D_MODEL, LANES), lambda i, j: (0, 0)),
            pl.BlockSpec((N_SMALL, D_MODEL), lambda i, j: (0, 0)),
        ],
        out_specs=[
            pl.BlockSpec((tm, tn), lambda i, j: (i, j)),
            pl.BlockSpec((tm, LANES), lambda i, j: (i, 0)),
            pl.BlockSpec((N_SMALL, tm), lambda i, j: (0, i)),
        ],
        out_shape=[
            jax.ShapeDtypeStruct((t, D_MAIN), BF16),
            jax.ShapeDtypeStruct((t, LANES), F32),
            jax.ShapeDtypeStruct((N_SMALL, t), F32),
        ],
        scratch_shapes=[pltpu.VMEM((tm, D_MODEL), BF16)],
        compiler_params=_params("parallel", "arbitrary"),
        name="inproj",
    )(x, g, w_main, w_small, w_small_t)


N_PIECES = 3


def _extra_lane(head):
    return (head // 2) * LANES + (FOX_HEAD_DIM if head % 2 == 0 else 0)


def _fprep_kernel(f_ref, fb_ref, tril_ref, place_ref, out_ref, carry):
    @pl.when(pl.program_id(1) == 0)
    def _():
        carry[...] = jnp.zeros_like(carry)

    tl = f_ref.shape[0]
    lane = lax.broadcasted_iota(jnp.int32, (tl, LANES), 1)
    is_f = (lane >= SSD_HEADS) & (lane < N_SMALL)
    logf = jnp.where(is_f, -_softplus(-(f_ref[...] + fb_ref[...])), 0.0)
    cum = _dot_sel_left(tril_ref[...], logf) + carry[0:1, :]
    carry[...] = jnp.broadcast_to(cum[tl - 1:tl, :], carry.shape)
    pieces = _split3(cum)
    out = _dot(pieces[0], place_ref[0])
    for i in range(1, N_PIECES):
        out = out + _dot(pieces[i], place_ref[i])
    out_ref[...] = out.astype(BF16)


def _fprep(small, f_bias, bsz, seq, tl):
    nl = seq // tl
    fb = jnp.zeros((1, LANES), F32).at[0, SSD_HEADS:N_SMALL].set(f_bias.astype(F32))
    r = lax.broadcasted_iota(jnp.int32, (tl, tl), 0)
    c = lax.broadcasted_iota(jnp.int32, (tl, tl), 1)
    tril = (c <= r).astype(BF16)
    heads = jnp.arange(FOX_HEADS)
    lanes = jnp.asarray([_extra_lane(h) for h in range(FOX_HEADS)])
    place = jnp.zeros((N_PIECES, LANES, D_FOX), F32)
    for i in range(N_PIECES):
        place = place.at[i, SSD_HEADS + heads, lanes + i].set(-1.0)
    return pl.pallas_call(
        _fprep_kernel,
        grid=(bsz, nl),
        in_specs=[
            pl.BlockSpec((tl, LANES), lambda b, j: (b * nl + j, 0)),
            pl.BlockSpec((1, LANES), lambda b, j: (0, 0)),
            pl.BlockSpec((tl, tl), lambda b, j: (0, 0)),
            pl.BlockSpec((N_PIECES, LANES, D_FOX), lambda b, j: (0, 0, 0)),
        ],
        out_specs=pl.BlockSpec((tl, D_FOX), lambda b, j: (b * nl + j, 0)),
        out_shape=jax.ShapeDtypeStruct((bsz * seq, D_FOX), BF16),
        scratch_shapes=[pltpu.VMEM((SUBLANES, LANES), F32)],
        compiler_params=_params("parallel", "arbitrary"),
        name="fprep",
    )(small, fb, tril, place.astype(BF16))


def _ssd_kernel(xbc_ref, z_ref, dt_ref, dtt_ref, cw_ref, cb_ref, dtb_ref, alog_ref, dtbc_ref,
                alogc_ref, exp_ref, dsk_ref, gn_ref, tril_ref, triu_ref, out_ref, xpad, state):
    pad = SUBLANES

    @pl.when(pl.program_id(1) == 0)
    def _():
        xpad[0:pad, :] = jnp.zeros((pad, CONV_CH), F32)
        state[...] = jnp.zeros_like(state)

    xpad[pad:pad + CHUNK, :] = xbc_ref[...].astype(F32)
    conv = cb_ref[...]
    for k in range(CONV_WIDTH):
        off = pad - (CONV_WIDTH - 1) + k
        conv = conv + cw_ref[k:k + 1, :] * xpad[off:off + CHUNK, :]
    xpad[0:pad, :] = xpad[CHUNK:CHUNK + pad, :]
    u = _silu(conv)
    xs = u[:, :D_SSD]
    gs = SSD_GROUPS * SSD_STATE
    bm = u[:, D_SSD:D_SSD + gs]
    cm = u[:, D_SSD + gs:]

    expand = exp_ref[...]
    dt = _softplus(dt_ref[...] + dtb_ref[...])
    a = -jnp.exp(alog_ref[...])
    acs = _dot_sel_left(tril_ref[...], dt * a)
    dtx = _dot_sel_right(dt, expand)
    acsx = _dot_sel_right(acs, expand)
    acs_last_x = acsx[CHUNK - 1:CHUNK, :]
    dtt = _softplus(dtt_ref[...] + dtbc_ref[...])
    at = -jnp.exp(alogc_ref[...])
    acst = _dot_sel_right(dtt * at, triu_ref[...])

    xc = xs * dtx
    xc_b = xc.astype(BF16)
    xdec_b = (xc * jnp.exp(acs_last_x - acsx)).astype(BF16)
    prev_b = state[...].astype(BF16)

    row = lax.broadcasted_iota(jnp.int32, (CHUNK, CHUNK), 0)
    col = lax.broadcasted_iota(jnp.int32, (CHUNK, CHUNK), 1)
    causal = col <= row
    low_half = lax.broadcasted_iota(jnp.int32, (CHUNK, LANES), 1) < SSD_HEAD_DIM
    heads_per_group = SSD_HEADS // SSD_GROUPS
    gw = heads_per_group * SSD_HEAD_DIM

    y_diag = []
    y_off = []
    st_new = []
    for g in range(SSD_GROUPS):
        bg = bm[:, g * SSD_STATE:(g + 1) * SSD_STATE]
        cg = cm[:, g * SSD_STATE:(g + 1) * SSD_STATE].astype(BF16)
        cbg = _dot_nt(cg, bg.astype(BF16))
        yd = []
        for r in range(heads_per_group):
            h = g * heads_per_group + r
            seg = acs[:, h:h + 1] - acst[h:h + 1, :]
            dec = jnp.where(causal, jnp.exp(jnp.where(causal, seg, 0.0)), 0.0)
            m = (cbg * dec).astype(BF16)
            j = h // 2
            yd.append(_dot(m, xc_b[:, j * LANES:(j + 1) * LANES]))
        for r in range(0, heads_per_group, 2):
            y_diag.append(jnp.where(low_half, yd[r], yd[r + 1]))
        sl = slice(g * gw, (g + 1) * gw)
        st_new.append(_dot(bg.T.astype(BF16), xdec_b[:, sl]))
        y_off.append(_dot(cg, prev_b[:, sl]))
    y_diag = jnp.concatenate(y_diag, axis=1)
    y_off = jnp.concatenate(y_off, axis=1) * jnp.exp(acsx)
    state[...] = state[...] * jnp.exp(acs_last_x) + jnp.concatenate(st_new, axis=1)

    y = y_diag + y_off + dsk_ref[...] * xs
    yg = y * _silu(z_ref[...].astype(F32))
    out_ref[...] = _rms(yg, gn_ref[...]).astype(BF16)


def _ssd(proj, small, small_t, conv_w, conv_b, dt_bias, a_log, d_skip, ssd_norm, bsz, seq):
    t = bsz * seq
    nc = seq // CHUNK
    pad_row = lambda v: jnp.zeros((1, LANES), F32).at[0, :SSD_HEADS].set(v.astype(F32))
    col = lambda v: jnp.broadcast_to(v.astype(F32)[:, None], (SSD_HEADS, CHUNK))
    hh = lax.broadcasted_iota(jnp.int32, (LANES, D_SSD), 0)
    cc = lax.broadcasted_iota(jnp.int32, (LANES, D_SSD), 1)
    expand = (cc // SSD_HEAD_DIM == hh).astype(BF16)
    r = lax.broadcasted_iota(jnp.int32, (CHUNK, CHUNK), 0)
    c = lax.broadcasted_iota(jnp.int32, (CHUNK, CHUNK), 1)
    tril = (c <= r).astype(BF16)
    triu = (r <= c).astype(BF16)
    dsk = jnp.repeat(d_skip.astype(F32), SSD_HEAD_DIM)[None, :]
    const = lambda shape: pl.BlockSpec(shape, lambda b, j: (0, 0))
    return pl.pallas_call(
        _ssd_kernel,
        grid=(bsz, nc),
        in_specs=[
            pl.BlockSpec((CHUNK, CONV_CH), lambda b, j: (b * nc + j, 0)),
            pl.BlockSpec((CHUNK, D_SSD), lambda b, j: (b * nc + j, COL_Z // D_SSD)),
            pl.BlockSpec((CHUNK, LANES), lambda b, j: (b * nc + j, 0)),
            pl.BlockSpec((SSD_HEADS, CHUNK), lambda b, j: (0, b * nc + j)),
            const((CONV_WIDTH, CONV_CH)),
            const((1, CONV_CH)),
            const((1, LANES)),
            const((1, LANES)),
            const((SSD_HEADS, CHUNK)),
            const((SSD_HEADS, CHUNK)),
            const((LANES, D_SSD)),
            const((1, D_SSD)),
            const((1, D_SSD)),
            const((CHUNK, CHUNK)),
            const((CHUNK, CHUNK)),
        ],
        out_specs=pl.BlockSpec((CHUNK, D_SSD), lambda b, j: (b * nc + j, 0)),
        out_shape=jax.ShapeDtypeStruct((t, D_SSD), BF16),
        scratch_shapes=[
            pltpu.VMEM((CHUNK + SUBLANES, CONV_CH), F32),
            pltpu.VMEM((SSD_STATE, D_SSD), F32),
        ],
        compiler_params=_params("parallel", "arbitrary"),
        name="ssd",
    )(proj, proj, small, small_t, conv_w.astype(F32), conv_b.astype(F32)[None, :], pad_row(dt_bias),
      pad_row(a_log), col(dt_bias), col(a_log), expand, dsk, ssd_norm.astype(F32)[None, :], tril, triu)


ATT_SUB_Q = 512
ATT_MAX_Q = 1024
ATT_SUB_K = 512


def _attn_kernel(qi_tab, ki_tab, q_ref, k_ref, v_ref, fx_ref, o_ref, q_scr, m_scr, acc_scr):
    p = pl.program_id(2)
    qi = qi_tab[p]
    ki = ki_tab[p]
    tq = q_ref.shape[0]
    sub_q = min(ATT_SUB_Q, tq)
    sub_k = min(ATT_SUB_K, tq)
    lane_q = lax.broadcasted_iota(jnp.int32, (tq, LANES), 1)
    lane = lax.broadcasted_iota(jnp.int32, (sub_k, LANES), 1)
    low = lane < FOX_HEAD_DIM
    own = (low, jnp.logical_not(low))
    extra0 = (FOX_HEAD_DIM, 0)

    @pl.when(ki == 0)
    def _():
        q = q_ref[...].astype(F32) * (FOX_HEAD_DIM ** -0.5)
        for h in range(2):
            is_one = (lane_q >= extra0[h]) & (lane_q < extra0[h] + N_PIECES)
            mine = (lane_q < FOX_HEAD_DIM) if h == 0 else (lane_q >= FOX_HEAD_DIM)
            q_scr[h] = jnp.where(mine, q, jnp.where(is_one, 1.0, 0.0)).astype(BF16)
        m_scr[...] = jnp.full(m_scr.shape, -jnp.inf, F32)
        acc_scr[...] = jnp.zeros_like(acc_scr)

    def key_block(ks, diagonal):
        k0 = ks * sub_k
        rk = pl.ds(k0, sub_k)
        k = k_ref[rk, :]
        v = v_ref[rk, :]
        fx = fx_ref[rk, :]
        for h in range(2):
            k_aug = jnp.where(own[h], k, fx)
            v_aug = jnp.where(own[h], v, jnp.where(lane == extra0[h], 1.0, 0.0).astype(BF16))
            q0 = 0
            while q0 < tq:
                if diagonal and k0 > q0 + sub_q - 1:
                    q0 += sub_q
                    continue
                rows = sub_q
                while (rows < min(ATT_MAX_Q, tq) and q0 % (2 * rows) == 0 and q0 + 2 * rows <= tq
                       and (not diagonal or k0 + sub_k - 1 <= q0)):
                    rows *= 2
                rq = pl.ds(q0, rows)
                s = _dot_nt(q_scr[h, rq, :], k_aug)
                if diagonal and k0 + sub_k - 1 > q0:
                    row = q0 + lax.broadcasted_iota(jnp.int32, (rows, sub_k), 0)
                    col = k0 + lax.broadcasted_iota(jnp.int32, (rows, sub_k), 1)
                    s = jnp.where(col <= row, s, -jnp.inf)
                q0 += rows
                tiles = [s[:, j * LANES:(j + 1) * LANES] for j in range(sub_k // LANES)]
                m_tile = functools.reduce(jnp.maximum, tiles)
                m_prev = m_scr[h, rq, :]
                m_new = jnp.maximum(m_prev, jnp.max(m_tile, axis=-1, keepdims=True))
                alpha = jnp.exp(m_prev - m_new)
                pr = jnp.concatenate([jnp.exp((t - m_new).astype(BF16)) for t in tiles], axis=1)
                m_scr[h, rq, :] = m_new
                acc_scr[h, rq, :] = alpha * acc_scr[h, rq, :] + _dot(pr, v_aug)

    @pl.when(ki < qi)
    def _():
        for ks in range(tq // sub_k):
            key_block(ks, False)

    @pl.when(ki == qi)
    def _():
        for ks in range(tq // sub_k):
            key_block(ks, True)
        outs = []
        for h in range(2):
            acc = acc_scr[h]
            outs.append(acc / acc[:, extra0[h]:extra0[h] + 1])
        o_ref[...] = jnp.where(lane_q < FOX_HEAD_DIM, outs[0], outs[1]).astype(BF16)


def _attention(proj, fx, bsz, seq, tq):
    t = bsz * seq
    nq = seq // tq
    pairs = [(q, k) for q in range(nq) for k in range(q + 1)]
    qi_tab = jnp.asarray([p[0] for p in pairs], jnp.int32)
    ki_tab = jnp.asarray([p[1] for p in pairs], jnp.int32)
    n_hp = FOX_HEADS // 2
    cq, ck, cv = COL_Q // LANES, COL_K // LANES, COL_V // LANES
    grid_spec = pltpu.PrefetchScalarGridSpec(
        num_scalar_prefetch=2,
        grid=(bsz, n_hp, len(pairs)),
        in_specs=[
            pl.BlockSpec((tq, LANES), lambda b, h, p, qt, kt: (b * nq + qt[p], cq + h)),
            pl.BlockSpec((tq, LANES), lambda b, h, p, qt, kt: (b * nq + kt[p], ck + h)),
            pl.BlockSpec((tq, LANES), lambda b, h, p, qt, kt: (b * nq + kt[p], cv + h)),
            pl.BlockSpec((tq, LANES), lambda b, h, p, qt, kt: (b * nq + kt[p], h)),
        ],
        out_specs=pl.BlockSpec((tq, LANES), lambda b, h, p, qt, kt: (b * nq + qt[p], h)),
        scratch_shapes=[
            pltpu.VMEM((2, tq, LANES), BF16),
            pltpu.VMEM((2, tq, LANES), F32),
            pltpu.VMEM((2, tq, LANES), F32),
        ],
    )
    return pl.pallas_call(
        _attn_kernel,
        grid_spec=grid_spec,
        out_shape=jax.ShapeDtypeStruct((t, D_FOX), BF16),
        compiler_params=_params("parallel", "parallel", "arbitrary"),
        name="fox_attention",
    )(qi_tab, ki_tab, proj, proj, proj, fx)


def _outproj_kernel(moe, *refs):
    if moe:
        (ys_ref, yf_ref, x_ref, fg_ref, w_ref, ng_ref, wr_ref, ltri_ref,
         xo_ref, hn_ref, route_ref, cnt_ref, carry) = refs
    else:
        ys_ref, yf_ref, x_ref, fg_ref, w_ref, ng_ref, xo_ref, hn_ref = refs
    yf = _rms(yf_ref[...].astype(F32), fg_ref[...]).astype(BF16)
    mix = _dot(ys_ref[...], w_ref[0:D_SSD, :]) + _dot(yf, w_ref[D_SSD:D_MIX, :])
    x = x_ref[...] + mix
    xo_ref[...] = x
    hn = _rms(x, ng_ref[...])
    if not moe:
        hn_ref[...] = hn.astype(BF16)
        return
    hn_ref[...] = hn

    @pl.when(pl.program_id(0) == 0)
    def _():
        carry[...] = jnp.zeros_like(carry)

    tm = x.shape[0]
    wr = wr_ref[...]
    h_hi = hn.astype(BF16)
    h_mid = (hn - h_hi.astype(F32)).astype(BF16)
    w_hi = wr.astype(BF16)
    w_mid = (wr - w_hi.astype(F32)).astype(BF16)
    logits = _dot(h_hi, w_hi) + _dot(h_hi, w_mid) + _dot(h_mid, w_hi)
    lane = lax.broadcasted_iota(jnp.int32, (tm, LANES), 1)
    lg = jnp.where(lane < N_EXPERTS, logits, -jnp.inf)
    m1 = jnp.max(lg, axis=-1, keepdims=True)
    i1 = jnp.min(jnp.where(lg == m1, lane, LANES), axis=-1, keepdims=True)
    lg2 = jnp.where(lane == i1, -jnp.inf, lg)
    m2 = jnp.max(lg2, axis=-1, keepdims=True)
    i2 = jnp.min(jnp.where(lg2 == m2, lane, LANES), axis=-1, keepdims=True)
    e2 = jnp.exp(m2 - m1)
    w1 = 1.0 / (1.0 + e2)
    w2 = e2 / (1.0 + e2)
    hit1 = lane == i1
    hit2 = lane == i2
    onehot = jnp.where(hit1 | hit2, 1.0, 0.0)
    before = _dot(ltri_ref[...], onehot.astype(BF16)) + carry[0:1, :]
    rank1 = jnp.sum(jnp.where(hit1, before, 0.0), axis=-1, keepdims=True)
    rank2 = jnp.sum(jnp.where(hit2, before, 0.0), axis=-1, keepdims=True)
    total = carry[0:1, :] + jnp.sum(onehot, axis=0, keepdims=True)
    carry[...] = jnp.broadcast_to(total, carry.shape)
    cnt_ref[...] = jnp.broadcast_to(total, cnt_ref.shape)
    route = jnp.where(lane == 0, i1.astype(F32), 0.0)
    route = jnp.where(lane == 1, i2.astype(F32), route)
    route = jnp.where(lane == 2, w1, route)
    route = jnp.where(lane == 3, w2, route)
    route = jnp.where(lane == 4, rank1, route)
    route = jnp.where(lane == 5, rank2, route)
    route_ref[...] = route


def _outproj(y_ssd, y_fox, x, fox_norm, w_out, next_norm, tm, router_w=None):
    t = x.shape[0]
    moe = router_w is not None
    row = lambda i: (i, 0)
    const = lambda i: (0, 0)
    in_specs = [
        pl.BlockSpec((tm, D_SSD), row),
        pl.BlockSpec((tm, D_FOX), row),
        pl.BlockSpec((tm, D_MODEL), row),
        pl.BlockSpec((1, D_FOX), const),
        pl.BlockSpec((D_MIX, D_MODEL), const),
        pl.BlockSpec((1, D_MODEL), const),
    ]
    args = [y_ssd, y_fox, x, fox_norm.astype(F32)[None, :], w_out.astype(BF16),
            next_norm.astype(F32)[None, :]]
    out_specs = [pl.BlockSpec((tm, D_MODEL), row), pl.BlockSpec((tm, D_MODEL), row)]
    out_shape = [jax.ShapeDtypeStruct((t, D_MODEL), F32),
                 jax.ShapeDtypeStruct((t, D_MODEL), F32 if moe else BF16)]
    scratch = []
    if moe:
        wr = jnp.zeros((D_MODEL, LANES), F32).at[:, :N_EXPERTS].set(router_w.astype(F32))
        r = lax.broadcasted_iota(jnp.int32, (tm, tm), 0)
        c = lax.broadcasted_iota(jnp.int32, (tm, tm), 1)
        ltri = (c < r).astype(BF16)
        in_specs += [pl.BlockSpec((D_MODEL, LANES), const), pl.BlockSpec((tm, tm), const)]
        args += [wr, ltri]
        out_specs += [pl.BlockSpec((tm, LANES), row), pl.BlockSpec((SUBLANES, LANES), const)]
        out_shape += [jax.ShapeDtypeStruct((t, LANES), F32),
                      jax.ShapeDtypeStruct((SUBLANES, LANES), F32)]
        scratch = [pltpu.VMEM((SUBLANES, LANES), F32)]
    return pl.pallas_call(
        functools.partial(_outproj_kernel, moe),
        grid=(t // tm,),
        in_specs=in_specs,
        out_specs=out_specs,
        out_shape=out_shape,
        scratch_shapes=scratch,
        compiler_params=_params("arbitrary" if moe else "parallel"),
        name="outproj_moe" if moe else "outproj",
    )(*args)


def _swiglu_kernel(residual, n_chunks, te_ref, nu_ref, *refs):
    if residual:
        h_ref, x_ref, wg_ref, wu_ref, wd_ref, o_ref = refs
    else:
        h_ref, wg_ref, wu_ref, wd_ref, o_ref = refs
    i = pl.program_id(0)

    @pl.when(i < nu_ref[0])
    def _():
        h = h_ref[...].astype(BF16)
        fc = D_FF // n_chunks
        acc = x_ref[...] if residual else None
        for c in range(n_chunks):
            g = _dot(h, wg_ref[0, :, c * fc:(c + 1) * fc])
            u = _dot(h, wu_ref[0, :, c * fc:(c + 1) * fc])
            a = (_silu(g) * u).astype(BF16)
            d = _dot(a, wd_ref[0, c * fc:(c + 1) * fc, :])
            acc = d if acc is None else acc + d
        o_ref[...] = acc

    @pl.when(i >= nu_ref[0])
    def _():
        o_ref[...] = jnp.zeros_like(o_ref)


def _swiglu(rows, tile_expert, n_used, w_gate, w_up, w_down, tm, x=None):
    r = rows.shape[0]
    residual = x is not None
    row = lambda i, te, nu: (i, 0)
    wspec = lambda shape: pl.BlockSpec(shape, lambda i, te, nu: (te[i], 0, 0),
                                       pipeline_mode=pl.Buffered(1))
    in_specs = [pl.BlockSpec((tm, D_MODEL), row)]
    args = [rows]
    if residual:
        in_specs.append(pl.BlockSpec((tm, D_MODEL), row))
        args.append(x)
    in_specs += [wspec((1, D_MODEL, D_FF)), wspec((1, D_MODEL, D_FF)), wspec((1, D_FF, D_MODEL))]
    args += [w_gate, w_up, w_down]
    grid_spec = pltpu.PrefetchScalarGridSpec(
        num_scalar_prefetch=2,
        grid=(r // tm,),
        in_specs=in_specs,
        out_specs=pl.BlockSpec((tm, D_MODEL), row),
    )
    return pl.pallas_call(
        functools.partial(_swiglu_kernel, residual, 2),
        grid_spec=grid_spec,
        out_shape=jax.ShapeDtypeStruct((r, D_MODEL), F32),
        compiler_params=_params("arbitrary"),
        name="swiglu_dense" if residual else "swiglu_experts",
    )(tile_expert, n_used, *args)


ROW_DMA_UNROLL = 8


def _dispatch_kernel(pos_ref, h_ref, buf_ref, o_ref, sem):
    del buf_ref
    tm = h_ref.shape[0]

    def row_copy(r, slot):
        dst = pos_ref[0, 0, 2 * r + slot]
        return pltpu.make_async_copy(h_ref.at[pl.ds(r, 1), :], o_ref.at[pl.ds(dst, 1), :], sem)

    def issue(r, carry):
        row_copy(r, 0).start()
        row_copy(r, 1).start()
        return carry

    def drain(r, carry):
        row_copy(r, 0).wait()
        row_copy(r, 1).wait()
        return carry

    lax.fori_loop(0, tm, issue, 0, unroll=ROW_DMA_UNROLL)
    lax.fori_loop(0, tm, drain, 0, unroll=ROW_DMA_UNROLL)


def _dispatch(hn, pos, r_pad, tm):
    t = hn.shape[0]
    pos3 = pos.reshape(t // tm, 1, 2 * tm)
    buf = jnp.zeros((r_pad, D_MODEL), F32)
    return pl.pallas_call(
        _dispatch_kernel,
        grid=(t // tm,),
        in_specs=[
            pl.BlockSpec((1, 1, 2 * tm), lambda i: (i, 0, 0), memory_space=pltpu.SMEM),
            pl.BlockSpec((tm, D_MODEL), lambda i: (i, 0)),
            pl.BlockSpec(memory_space=pl.ANY),
        ],
        out_specs=pl.BlockSpec(memory_space=pl.ANY),
        out_shape=jax.ShapeDtypeStruct((r_pad, D_MODEL), F32),
        scratch_shapes=[pltpu.SemaphoreType.DMA],
        input_output_aliases={2: 0},
        compiler_params=_params("arbitrary"),
        name="moe_dispatch",
    )(pos3, hn, buf)


def _combine_kernel(pos_ref, next_pos_ref, route_ref, x_ref, g_ref, y_ref, o_ref, buf, sem):
    tm = x_ref.shape[0]
    i = pl.program_id(0)
    par = i % 2

    def row_copy(idx_ref, r, slot, par):
        src = idx_ref[0, 0, 2 * r + slot]
        return pltpu.make_async_copy(y_ref.at[pl.ds(src, 1), :], buf.at[par, slot, pl.ds(r, 1), :],
                                     sem.at[par])

    def issue_all(idx_ref, par):
        def issue(r, carry):
            row_copy(idx_ref, r, 0, par).start()
            row_copy(idx_ref, r, 1, par).start()
            return carry
        lax.fori_loop(0, tm, issue, 0, unroll=ROW_DMA_UNROLL)

    def drain(r, carry):
        row_copy(pos_ref, r, 0, par).wait()
        row_copy(pos_ref, r, 1, par).wait()
        return carry

    @pl.when(i == 0)
    def _():
        issue_all(pos_ref, 0)

    @pl.when(i + 1 < pl.num_programs(0))
    def _():
        issue_all(next_pos_ref, 1 - par)

    lax.fori_loop(0, tm, drain, 0, unroll=ROW_DMA_UNROLL)
    route = route_ref[...]
    x = x_ref[...] + route[:, 2:3] * buf[par, 0] + route[:, 3:4] * buf[par, 1]
    o_ref[...] = _rms(x, g_ref[...])


def _combine(pos, route, x, final_norm, y_sorted, tm):
    t = x.shape[0]
    n = t // tm
    pos3 = pos.reshape(n, 1, 2 * tm)
    return pl.pallas_call(
        _combine_kernel,
        grid=(n,),
        in_specs=[
            pl.BlockSpec((1, 1, 2 * tm), lambda i: (i, 0, 0), memory_space=pltpu.SMEM),
            pl.BlockSpec((1, 1, 2 * tm), lambda i: (jnp.minimum(i + 1, n - 1), 0, 0),
                         memory_space=pltpu.SMEM),
            pl.BlockSpec((tm, LANES), lambda i: (i, 0)),
            pl.BlockSpec((tm, D_MODEL), lambda i: (i, 0)),
            pl.BlockSpec((1, D_MODEL), lambda i: (0, 0)),
            pl.BlockSpec(memory_space=pl.ANY),
        ],
        out_specs=pl.BlockSpec((tm, D_MODEL), lambda i: (i, 0)),
        out_shape=jax.ShapeDtypeStruct((t, D_MODEL), F32),
        scratch_shapes=[pltpu.VMEM((2, 2, tm, D_MODEL), F32), pltpu.SemaphoreType.DMA((2,))],
        compiler_params=_params("arbitrary"),
        name="moe_combine",
    )(pos3, pos3, route, x, final_norm.astype(F32)[None, :], y_sorted)


def _tile(n, want):
    t = min(n, want)
    assert n % t == 0
    return t


def _split_w_in(w):
    sizes = (D_SSD, CONV_CH, SSD_HEADS, D_FOX, D_FOX, D_FOX, FOX_HEADS)
    offs = [0]
    for s in sizes:
        offs.append(offs[-1] + s)
    z, xbc, dt, q, k, v, f = (w[:, offs[i]:offs[i + 1]] for i in range(7))
    w_main = jnp.concatenate([xbc, z, q, k, v], axis=1).astype(BF16)
    small = jnp.concatenate([dt, f], axis=1)
    w_small = jnp.zeros((D_MODEL, LANES), F32).at[:, :N_SMALL].set(small).astype(BF16)
    return w_main, w_small, small.T.astype(BF16)


def kernel(x, mix_norm, w_in, conv_w, conv_b, dt_bias, a_log, d_skip, ssd_norm, fox_f_bias, fox_norm,
           w_out, ffn_norm, ffn_w_gate, ffn_w_up, ffn_w_down, router_w, moe_w_gate, moe_w_up,
           moe_w_down, final_norm):
    bsz, seq, _ = x.shape
    t = bsz * seq
    depth = mix_norm.shape[0]
    assert depth == 2 and seq % CHUNK == 0
    x = x.reshape(t, D_MODEL).astype(F32)
    tm_proj = _tile(t, 1024)
    tm_out = _tile(t, 512)
    tm_ffn = _tile(t, 512)
    tm_moe = _tile(t, 512)
    tm_row = _tile(t, 256)
    tq = _tile(seq, 2048)

    def mixer(i, x, next_norm, router=None):
        w_main, w_small, w_small_t = _split_w_in(w_in[i])
        proj, small, small_t = _inproj(x, mix_norm[i].astype(F32)[None, :], w_main, w_small, w_small_t,
                                       tm_proj, 2048)
        y_ssd = _ssd(proj, small, small_t, conv_w[i], conv_b[i], dt_bias[i], a_log[i], d_skip[i],
                     ssd_norm[i], bsz, seq)
        fx = _fprep(small, fox_f_bias[i], bsz, seq, _tile(seq, 512))
        y_fox = _attention(proj, fx, bsz, seq, tq)
        return _outproj(y_ssd, y_fox, x, fox_norm[i], w_out[i], next_norm, tm_out, router)

    x, hn = mixer(0, x, ffn_norm[0])
    n_tiles = t // tm_ffn
    x = _swiglu(hn, jnp.zeros((n_tiles,), jnp.int32), jnp.full((1,), n_tiles, jnp.int32),
                ffn_w_gate.astype(BF16), ffn_w_up.astype(BF16), ffn_w_down.astype(BF16), tm_ffn, x=x)

    x, hn, route, counts = mixer(1, x, ffn_norm[1], router_w[0])
    cnt = counts[0, :N_EXPERTS].astype(jnp.int32)
    group = ((cnt + tm_moe - 1) // tm_moe) * tm_moe
    ends = jnp.cumsum(group)
    starts = ends - group
    idx = route[:, 0:2].astype(jnp.int32)
    rank = route[:, 4:6].astype(jnp.int32)
    pos = (jnp.take(starts, idx) + rank).astype(jnp.int32)
    n_moe_tiles = (2 * t) // tm_moe + N_EXPERTS
    r_pad = n_moe_tiles * tm_moe
    tile_start = jnp.arange(n_moe_tiles, dtype=jnp.int32) * tm_moe
    tile_expert = jnp.minimum(jnp.sum(tile_start[:, None] >= ends[None, :], axis=1), N_EXPERTS - 1)
    n_used = (ends[-1:] // tm_moe).astype(jnp.int32)
    rows = _dispatch(hn, pos, r_pad, tm_row)
    y_sorted = _swiglu(rows, tile_expert.astype(jnp.int32), n_used, moe_w_gate[0].astype(BF16),
                       moe_w_up[0].astype(BF16), moe_w_down[0].astype(BF16), tm_moe)
    out = _combine(pos, route, x, final_norm, y_sorted, tm_row)
    return out.reshape(bsz, seq, D_MODEL)
```

```python
import functools
import math

import jax
import jax.numpy as jnp
from jax import lax
from jax.experimental import pallas as pl
from jax.experimental.pallas import tpu as pltpu

F32 = jnp.float32
BF16 = jnp.bfloat16

D_MODEL = 1024
D_SSD = 1024
SSD_HEAD_DIM = 64
SSD_HEADS = 16
SSD_GROUPS = 4
SSD_STATE = 128
CONV_WIDTH = 4
CHUNK = 128
CONV_CH = D_SSD + 2 * SSD_GROUPS * SSD_STATE
D_FOX = 1024
FOX_HEAD_DIM = 64
FOX_HEADS = 16
D_MIX = D_SSD + D_FOX
D_FF = 2816
N_EXPERTS = 8
EPS = 1e-5

LANES = 128
SUBLANES = 8
VMEM_LIMIT_BYTES = 56 * 1024 * 1024

D_MAIN = CONV_CH + D_SSD + 3 * D_FOX
COL_Z = CONV_CH
COL_Q = COL_Z + D_SSD
COL_K = COL_Q + D_FOX
COL_V = COL_K + D_FOX
N_SMALL = SSD_HEADS + FOX_HEADS


def _params(*sem):
    return pltpu.CompilerParams(dimension_semantics=sem, vmem_limit_bytes=VMEM_LIMIT_BYTES)


def _split3(v):
    hi = v.astype(BF16)
    r1 = v - hi.astype(F32)
    mid = r1.astype(BF16)
    lo = (r1 - mid.astype(F32)).astype(BF16)
    return hi, mid, lo


def _dot(a, b):
    return jnp.dot(a, b, preferred_element_type=F32)


def _dot_nt(a, b):
    return lax.dot_general(a, b, (((1,), (1,)), ((), ())), preferred_element_type=F32)


def _dot_sel_right(v, sel_bf16):
    hi, mid, lo = _split3(v)
    return _dot(hi, sel_bf16) + _dot(mid, sel_bf16) + _dot(lo, sel_bf16)


def _dot_sel_left(sel_bf16, v):
    hi, mid, lo = _split3(v)
    return _dot(sel_bf16, hi) + _dot(sel_bf16, mid) + _dot(sel_bf16, lo)


def _softplus(x):
    return jnp.maximum(x, 0.0) + jnp.log1p(jnp.exp(-jnp.abs(x)))


def _silu(x):
    return x * jax.nn.sigmoid(x)


def _rms(x, g):
    ms = jnp.mean(x * x, axis=-1, keepdims=True)
    return (x * lax.rsqrt(ms + EPS)) * g


def _inproj_kernel(x_ref, g_ref, w_ref, ws_ref, wst_ref, proj_ref, small_ref, smallt_ref, hn_scr):
    @pl.when(pl.program_id(1) == 0)
    def _():
        hb = _rms(x_ref[...], g_ref[...]).astype(BF16)
        hn_scr[...] = hb
        small_ref[...] = _dot(hb, ws_ref[...])
        smallt_ref[...] = _dot_nt(wst_ref[...], hb)

    proj_ref[...] = _dot(hn_scr[...], w_ref[...]).astype(BF16)


def _inproj(x, g, w_main, w_small, w_small_t, tm, tn):
    t = x.shape[0]
    return pl.pallas_call(
        _inproj_kernel,
        grid=(t // tm, D_MAIN // tn),
        in_specs=[
            pl.BlockSpec((tm, D_MODEL), lambda i, j: (i, 0)),
            pl.BlockSpec((1, D_MODEL), lambda i, j: (0, 0)),
            pl.BlockSpec((D_MODEL, tn), lambda i, j: (0, j)),
            pl.BlockSpec((D_MODEL, LANES), lambda i, j: (0, 0)),
            pl.BlockSpec((N_SMALL, D_MODEL), lambda i, j: (0, 0)),
        ],
        out_specs=[
            pl.BlockSpec((tm, tn), lambda i, j: (i, j)),
            pl.BlockSpec((tm, LANES), lambda i, j: (i, 0)),
            pl.BlockSpec((N_SMALL, tm), lambda i, j: (0, i)),
        ],
        out_shape=[
            jax.ShapeDtypeStruct((t, D_MAIN), BF16),
            jax.ShapeDtypeStruct((t, LANES), F32),
            jax.ShapeDtypeStruct((N_SMALL, t), F32),
        ],
        scratch_shapes=[pltpu.VMEM((tm, D_MODEL), BF16)],
        compiler_params=_params("parallel", "arbitrary"),
        name="inproj",
    )(x, g, w_main, w_small, w_small_t)


N_PIECES = 3


def _extra_lane(head):
    return (head // 2) * LANES + (FOX_HEAD_DIM if head % 2 == 0 else 0)


def _fprep_kernel(f_ref, fb_ref, tril_ref, place_ref, out_ref, carry):
    @pl.when(pl.program_id(1) == 0)
    def _():
        carry[...] = jnp.zeros_like(carry)

    tl = f_ref.shape[0]
    lane = lax.broadcasted_iota(jnp.int32, (tl, LANES), 1)
    is_f = (lane >= SSD_HEADS) & (lane < N_SMALL)
    logf = jnp.where(is_f, -_softplus(-(f_ref[...] + fb_ref[...])), 0.0)
    cum = _dot_sel_left(tril_ref[...], logf) + carry[0:1, :]
    carry[...] = jnp.broadcast_to(cum[tl - 1:tl, :], carry.shape)
    pieces = _split3(cum)
    out = _dot(pieces[0], place_ref[0])
    for i in range(1, N_PIECES):
        out = out + _dot(pieces[i], place_ref[i])
    out_ref[...] = out.astype(BF16)


def _fprep(small, f_bias, bsz, seq, tl):
    nl = seq // tl
    fb = jnp.zeros((1, LANES), F32).at[0, SSD_HEADS:N_SMALL].set(f_bias.astype(F32))
    r = lax.broadcasted_iota(jnp.int32, (tl, tl), 0)
    c = lax.broadcasted_iota(jnp.int32, (tl, tl), 1)
    tril = (c <= r).astype(BF16)
    heads = jnp.arange(FOX_HEADS)
    lanes = jnp.asarray([_extra_lane(h) for h in range(FOX_HEADS)])
    place = jnp.zeros((N_PIECES, LANES, D_FOX), F32)
    for i in range(N_PIECES):
        place = place.at[i, SSD_HEADS + heads, lanes + i].set(-1.0)
    return pl.pallas_call(
        _fprep_kernel,
        grid=(bsz, nl),
        in_specs=[
            pl.BlockSpec((tl, LANES), lambda b, j: (b * nl + j, 0)),
            pl.BlockSpec((1, LANES), lambda b, j: (0, 0)),
            pl.BlockSpec((tl, tl), lambda b, j: (0, 0)),
            pl.BlockSpec((N_PIECES, LANES, D_FOX), lambda b, j: (0, 0, 0)),
        ],
        out_specs=pl.BlockSpec((tl, D_FOX), lambda b, j: (b * nl + j, 0)),
        out_shape=jax.ShapeDtypeStruct((bsz * seq, D_FOX), BF16),
        scratch_shapes=[pltpu.VMEM((SUBLANES, LANES), F32)],
        compiler_params=_params("parallel", "arbitrary"),
        name="fprep",
    )(small, fb, tril, place.astype(BF16))


def _ssd_kernel(xbc_ref, dt_ref, dtt_ref, cw_ref, cb_ref, dtb_ref, alog_ref, dtbc_ref,
                alogc_ref, exp_ref, dsk_ref, tril_ref, triu_ref, out_ref, xtail, ytail, state):
    pad = SUBLANES
    assert CONV_WIDTH == 4

    @pl.when(pl.program_id(1) == 0)
    def _():
        xtail[...] = jnp.zeros_like(xtail)
        ytail[...] = jnp.zeros_like(ytail)
        state[...] = jnp.zeros_like(state)

    tile_row = lax.broadcasted_iota(jnp.int32, (pad, CONV_CH), 0)

    def delayed(cur, tail_ref, d):
        rolled = pltpu.roll(cur, d, 0)
        head = jnp.where(tile_row < d, pltpu.roll(tail_ref[...], d, 0), rolled[0:pad, :])
        return jnp.concatenate([head, rolled[pad:, :]], axis=0)

    x_now = xbc_ref[...].astype(F32)
    x_prev = delayed(x_now, xtail, 1)
    early = cw_ref[0:1, :] * x_prev + cw_ref[1:2, :] * x_now
    late = cw_ref[2:3, :] * x_prev + cw_ref[3:4, :] * x_now
    conv = (delayed(early, ytail, 2) + late) + cb_ref[...]
    xtail[...] = x_now[CHUNK - pad:, :]
    ytail[...] = early[CHUNK - pad:, :]
    u = _silu(conv)
    xs = u[:, :D_SSD]
    gs = SSD_GROUPS * SSD_STATE
    bm = u[:, D_SSD:D_SSD + gs]
    cm = u[:, D_SSD + gs:]

    expand = exp_ref[...]
    dt = _softplus(dt_ref[...] + dtb_ref[...])
    a = -jnp.exp(alog_ref[...])
    acs = _dot_sel_left(tril_ref[...], dt * a)
    dtx = _dot_sel_right(dt, expand)
    acsx = _dot_sel_right(acs, expand)
    acs_last_x = acsx[CHUNK - 1:CHUNK, :]
    dtt = _softplus(dtt_ref[...] + dtbc_ref[...])
    at = -jnp.exp(alogc_ref[...])
    acst = _dot_sel_right(dtt * at, triu_ref[...])

    xc = xs * dtx
    xc_b = xc.astype(BF16)
    xdec_b = (xc * jnp.exp(acs_last_x - acsx)).astype(BF16)
    prev_b = state[...].astype(BF16)

    row = lax.broadcasted_iota(jnp.int32, (CHUNK, CHUNK), 0)
    col = lax.broadcasted_iota(jnp.int32, (CHUNK, CHUNK), 1)
    causal = col <= row
    low_half = lax.broadcasted_iota(jnp.int32, (CHUNK, LANES), 1) < SSD_HEAD_DIM
    heads_per_group = SSD_HEADS // SSD_GROUPS
    gw = heads_per_group * SSD_HEAD_DIM

    y_diag = []
    y_off = []
    st_new = []
    for g in range(SSD_GROUPS):
        bg = bm[:, g * SSD_STATE:(g + 1) * SSD_STATE]
        cg = cm[:, g * SSD_STATE:(g + 1) * SSD_STATE].astype(BF16)
        cbg = _dot_nt(cg, bg.astype(BF16))
        yd = []
        for r in range(heads_per_group):
            h = g * heads_per_group + r
            seg = acs[:, h:h + 1] - acst[h:h + 1, :]
            dec = jnp.exp(jnp.where(causal, seg, -jnp.inf))
            m = (cbg * dec).astype(BF16)
            j = h // 2
            yd.append(_dot(m, xc_b[:, j * LANES:(j + 1) * LANES]))
        for r in range(0, heads_per_group, 2):
            y_diag.append(jnp.where(low_half, yd[r], yd[r + 1]))
        sl = slice(g * gw, (g + 1) * gw)
        st_new.append(_dot(bg.T.astype(BF16), xdec_b[:, sl]))
        y_off.append(_dot(cg, prev_b[:, sl]))
    y_diag = jnp.concatenate(y_diag, axis=1)
    y_off = jnp.concatenate(y_off, axis=1) * jnp.exp(acsx)
    state[...] = state[...] * jnp.exp(acs_last_x) + jnp.concatenate(st_new, axis=1)

    out_ref[...] = (y_diag + y_off + dsk_ref[...] * xs).astype(BF16)


def _ssd(proj, small, small_t, conv_w, conv_b, dt_bias, a_log, d_skip, bsz, seq):
    t = bsz * seq
    nc = seq // CHUNK
    pad_row = lambda v: jnp.zeros((1, LANES), F32).at[0, :SSD_HEADS].set(v.astype(F32))
    col = lambda v: jnp.broadcast_to(v.astype(F32)[:, None], (SSD_HEADS, CHUNK))
    hh = lax.broadcasted_iota(jnp.int32, (LANES, D_SSD), 0)
    cc = lax.broadcasted_iota(jnp.int32, (LANES, D_SSD), 1)
    expand = (cc // SSD_HEAD_DIM == hh).astype(BF16)
    r = lax.broadcasted_iota(jnp.int32, (CHUNK, CHUNK), 0)
    c = lax.broadcasted_iota(jnp.int32, (CHUNK, CHUNK), 1)
    tril = (c <= r).astype(BF16)
    triu = (r <= c).astype(BF16)
    dsk = jnp.repeat(d_skip.astype(F32), SSD_HEAD_DIM)[None, :]
    const = lambda shape: pl.BlockSpec(shape, lambda b, j: (0, 0))
    return pl.pallas_call(
        _ssd_kernel,
        grid=(bsz, nc),
        in_specs=[
            pl.BlockSpec((CHUNK, CONV_CH), lambda b, j: (b * nc + j, 0)),
            pl.BlockSpec((CHUNK, LANES), lambda b, j: (b * nc + j, 0)),
            pl.BlockSpec((SSD_HEADS, CHUNK), lambda b, j: (0, b * nc + j)),
            const((CONV_WIDTH, CONV_CH)),
            const((1, CONV_CH)),
            const((1, LANES)),
            const((1, LANES)),
            const((SSD_HEADS, CHUNK)),
            const((SSD_HEADS, CHUNK)),
            const((LANES, D_SSD)),
            const((1, D_SSD)),
            const((CHUNK, CHUNK)),
            const((CHUNK, CHUNK)),
        ],
        out_specs=pl.BlockSpec((CHUNK, D_SSD), lambda b, j: (b * nc + j, 0)),
        out_shape=jax.ShapeDtypeStruct((t, D_SSD), BF16),
        scratch_shapes=[
            pltpu.VMEM((SUBLANES, CONV_CH), F32),
            pltpu.VMEM((SUBLANES, CONV_CH), F32),
            pltpu.VMEM((SSD_STATE, D_SSD), F32),
        ],
        compiler_params=_params("parallel", "arbitrary"),
        name="ssd",
    )(proj, small, small_t, conv_w.astype(F32), conv_b.astype(F32)[None, :], pad_row(dt_bias),
      pad_row(a_log), col(dt_bias), col(a_log), expand, dsk, tril, triu)


ATT_SUB_Q = 512
ATT_MAX_Q = 1024
ATT_SUB_K = 512


def _attn_kernel(qi_tab, ki_tab, q_ref, k_ref, v_ref, fx_ref, o_ref, q_scr, m_scr, acc_scr):
    p = pl.program_id(2)
    qi = qi_tab[p]
    ki = ki_tab[p]
    tq = q_ref.shape[0]
    sub_q = min(ATT_SUB_Q, tq)
    sub_k = min(ATT_SUB_K, tq)
    lane_q = lax.broadcasted_iota(jnp.int32, (tq, LANES), 1)
    lane = lax.broadcasted_iota(jnp.int32, (sub_k, LANES), 1)
    low = lane < FOX_HEAD_DIM
    own = (low, jnp.logical_not(low))
    extra0 = (FOX_HEAD_DIM, 0)

    @pl.when(ki == 0)
    def _():
        q = q_ref[...].astype(F32) * (FOX_HEAD_DIM ** -0.5)
        for h in range(2):
            is_one = (lane_q >= extra0[h]) & (lane_q < extra0[h] + N_PIECES)
            mine = (lane_q < FOX_HEAD_DIM) if h == 0 else (lane_q >= FOX_HEAD_DIM)
            q_scr[h] = jnp.where(mine, q, jnp.where(is_one, 1.0, 0.0)).astype(BF16)
        m_scr[...] = jnp.full(m_scr.shape, -jnp.inf, F32)
        acc_scr[...] = jnp.zeros_like(acc_scr)

    def key_block(ks, diagonal):
        k0 = ks * sub_k
        rk = pl.ds(k0, sub_k)
        k = k_ref[rk, :]
        v = v_ref[rk, :]
        fx = fx_ref[rk, :]
        for h in range(2):
            k_aug = jnp.where(own[h], k, fx)
            v_aug = jnp.where(own[h], v, jnp.where(lane == extra0[h], 1.0, 0.0).astype(BF16))
            q0 = 0
            while q0 < tq:
                if diagonal and k0 > q0 + sub_q - 1:
                    q0 += sub_q
                    continue
                rows = sub_q
                while (rows < min(ATT_MAX_Q, tq) and q0 % (2 * rows) == 0 and q0 + 2 * rows <= tq
                       and (not diagonal or k0 + sub_k - 1 <= q0)):
                    rows *= 2
                rq = pl.ds(q0, rows)
                s = _dot_nt(q_scr[h, rq, :], k_aug)
                if diagonal and k0 + sub_k - 1 > q0:
                    row = q0 + lax.broadcasted_iota(jnp.int32, (rows, sub_k), 0)
                    col = k0 + lax.broadcasted_iota(jnp.int32, (rows, sub_k), 1)
                    s = jnp.where(col <= row, s, -jnp.inf)
                q0 += rows
                tiles = [s[:, j * LANES:(j + 1) * LANES] for j in range(sub_k // LANES)]
                m_tile = functools.reduce(jnp.maximum, tiles)
                m_prev = m_scr[h, rq, :]
                m_new = jnp.maximum(m_prev, jnp.max(m_tile, axis=-1, keepdims=True))
                alpha = jnp.exp(m_prev - m_new)
                pr = jnp.concatenate([jnp.exp((t - m_new).astype(BF16)) for t in tiles], axis=1)
                m_scr[h, rq, :] = m_new
                acc_scr[h, rq, :] = alpha * acc_scr[h, rq, :] + _dot(pr, v_aug)

    @pl.when(ki < qi)
    def _():
        for ks in range(tq // sub_k):
            key_block(ks, False)

    @pl.when(ki == qi)
    def _():
        for ks in range(tq // sub_k):
            key_block(ks, True)
        outs = []
        for h in range(2):
            acc = acc_scr[h]
            outs.append(acc / acc[:, extra0[h]:extra0[h] + 1])
        o_ref[...] = jnp.where(lane_q < FOX_HEAD_DIM, outs[0], outs[1]).astype(BF16)


def _attention(proj, fx, bsz, seq, tq):
    t = bsz * seq
    nq = seq // tq
    pairs = [(q, k) for q in range(nq) for k in range(q + 1)]
    qi_tab = jnp.asarray([p[0] for p in pairs], jnp.int32)
    ki_tab = jnp.asarray([p[1] for p in pairs], jnp.int32)
    n_hp = FOX_HEADS // 2
    cq, ck, cv = COL_Q // LANES, COL_K // LANES, COL_V // LANES
    grid_spec = pltpu.PrefetchScalarGridSpec(
        num_scalar_prefetch=2,
        grid=(bsz, n_hp, len(pairs)),
        in_specs=[
            pl.BlockSpec((tq, LANES), lambda b, h, p, qt, kt: (b * nq + qt[p], cq + h)),
            pl.BlockSpec((tq, LANES), lambda b, h, p, qt, kt: (b * nq + kt[p], ck + h)),
            pl.BlockSpec((tq, LANES), lambda b, h, p, qt, kt: (b * nq + kt[p], cv + h)),
            pl.BlockSpec((tq, LANES), lambda b, h, p, qt, kt: (b * nq + kt[p], h)),
        ],
        out_specs=pl.BlockSpec((tq, LANES), lambda b, h, p, qt, kt: (b * nq + qt[p], h)),
        scratch_shapes=[
            pltpu.VMEM((2, tq, LANES), BF16),
            pltpu.VMEM((2, tq, LANES), F32),
            pltpu.VMEM((2, tq, LANES), F32),
        ],
    )
    return pl.pallas_call(
        _attn_kernel,
        grid_spec=grid_spec,
        out_shape=jax.ShapeDtypeStruct((t, D_FOX), BF16),
        compiler_params=_params("parallel", "parallel", "arbitrary"),
        name="fox_attention",
    )(qi_tab, ki_tab, proj, proj, proj, fx)


def _outproj_kernel(moe, *refs):
    if moe:
        (ys_ref, z_ref, yf_ref, x_ref, sg_ref, fg_ref, w_ref, ng_ref, wr_ref, ltri_ref,
         xo_ref, hn_ref, route_ref, cnt_ref, carry) = refs
    else:
        ys_ref, z_ref, yf_ref, x_ref, sg_ref, fg_ref, w_ref, ng_ref, xo_ref, hn_ref = refs
    ys = _rms(ys_ref[...].astype(F32) * _silu(z_ref[...].astype(F32)), sg_ref[...]).astype(BF16)
    yf = _rms(yf_ref[...].astype(F32), fg_ref[...]).astype(BF16)
    mix = _dot(ys, w_ref[0:D_SSD, :]) + _dot(yf, w_ref[D_SSD:D_MIX, :])
    x = x_ref[...] + mix
    xo_ref[...] = x
    hn = _rms(x, ng_ref[...])
    if not moe:
        hn_ref[...] = hn.astype(BF16)
        return
    hn_ref[...] = hn

    @pl.when(pl.program_id(0) == 0)
    def _():
        carry[...] = jnp.zeros_like(carry)

    tm = x.shape[0]
    wr = wr_ref[...]
    h_hi = hn.astype(BF16)
    h_mid = (hn - h_hi.astype(F32)).astype(BF16)
    w_hi = wr.astype(BF16)
    w_mid = (wr - w_hi.astype(F32)).astype(BF16)
    logits = _dot(h_hi, w_hi) + _dot(h_hi, w_mid) + _dot(h_mid, w_hi)
    lane = lax.broadcasted_iota(jnp.int32, (tm, LANES), 1)
    lg = jnp.where(lane < N_EXPERTS, logits, -jnp.inf)
    m1 = jnp.max(lg, axis=-1, keepdims=True)
    i1 = jnp.min(jnp.where(lg == m1, lane, LANES), axis=-1, keepdims=True)
    lg2 = jnp.where(lane == i1, -jnp.inf, lg)
    m2 = jnp.max(lg2, axis=-1, keepdims=True)
    i2 = jnp.min(jnp.where(lg2 == m2, lane, LANES), axis=-1, keepdims=True)
    e2 = jnp.exp(m2 - m1)
    w1 = 1.0 / (1.0 + e2)
    w2 = e2 / (1.0 + e2)
    hit1 = lane == i1
    hit2 = lane == i2
    onehot = jnp.where(hit1 | hit2, 1.0, 0.0)
    before = _dot(ltri_ref[...], onehot.astype(BF16)) + carry[0:1, :]
    rank1 = jnp.sum(jnp.where(hit1, before, 0.0), axis=-1, keepdims=True)
    rank2 = jnp.sum(jnp.where(hit2, before, 0.0), axis=-1, keepdims=True)
    total = carry[0:1, :] + jnp.sum(onehot, axis=0, keepdims=True)
    carry[...] = jnp.broadcast_to(total, carry.shape)
    cnt_ref[...] = jnp.broadcast_to(total, cnt_ref.shape)
    route = jnp.where(lane == 0, i1.astype(F32), 0.0)
    route = jnp.where(lane == 1, i2.astype(F32), route)
    route = jnp.where(lane == 2, w1, route)
    route = jnp.where(lane == 3, w2, route)
    route = jnp.where(lane == 4, rank1, route)
    route = jnp.where(lane == 5, rank2, route)
    route_ref[...] = route


def _outproj(y_ssd, proj, y_fox, x, ssd_norm, fox_norm, w_out, next_norm, tm, router_w=None):
    t = x.shape[0]
    moe = router_w is not None
    row = lambda i: (i, 0)
    const = lambda i: (0, 0)
    in_specs = [
        pl.BlockSpec((tm, D_SSD), row),
        pl.BlockSpec((tm, D_SSD), lambda i: (i, COL_Z // D_SSD)),
        pl.BlockSpec((tm, D_FOX), row),
        pl.BlockSpec((tm, D_MODEL), row),
        pl.BlockSpec((1, D_SSD), const),
        pl.BlockSpec((1, D_FOX), const),
        pl.BlockSpec((D_MIX, D_MODEL), const),
        pl.BlockSpec((1, D_MODEL), const),
    ]
    args = [y_ssd, proj, y_fox, x, ssd_norm.astype(F32)[None, :], fox_norm.astype(F32)[None, :],
            w_out.astype(BF16), next_norm.astype(F32)[None, :]]
    out_specs = [pl.BlockSpec((tm, D_MODEL), row), pl.BlockSpec((tm, D_MODEL), row)]
    out_shape = [jax.ShapeDtypeStruct((t, D_MODEL), F32),
                 jax.ShapeDtypeStruct((t, D_MODEL), F32 if moe else BF16)]
    scratch = []
    if moe:
        wr = jnp.zeros((D_MODEL, LANES), F32).at[:, :N_EXPERTS].set(router_w.astype(F32))
        r = lax.broadcasted_iota(jnp.int32, (tm, tm), 0)
        c = lax.broadcasted_iota(jnp.int32, (tm, tm), 1)
        ltri = (c < r).astype(BF16)
        in_specs += [pl.BlockSpec((D_MODEL, LANES), const), pl.BlockSpec((tm, tm), const)]
        args += [wr, ltri]
        out_specs += [pl.BlockSpec((tm, LANES), row), pl.BlockSpec((SUBLANES, LANES), const)]
        out_shape += [jax.ShapeDtypeStruct((t, LANES), F32),
                      jax.ShapeDtypeStruct((SUBLANES, LANES), F32)]
        scratch = [pltpu.VMEM((SUBLANES, LANES), F32)]
    return pl.pallas_call(
        functools.partial(_outproj_kernel, moe),
        grid=(t // tm,),
        in_specs=in_specs,
        out_specs=out_specs,
        out_shape=out_shape,
        scratch_shapes=scratch,
        compiler_params=_params("arbitrary" if moe else "parallel"),
        name="outproj_moe" if moe else "outproj",
    )(*args)


def _swiglu_kernel(residual, n_chunks, te_ref, nu_ref, *refs):
    if residual:
        h_ref, x_ref, wg_ref, wu_ref, wd_ref, o_ref = refs
    else:
        h_ref, wg_ref, wu_ref, wd_ref, o_ref = refs
    i = pl.program_id(0)

    @pl.when(i < nu_ref[0])
    def _():
        h = h_ref[...].astype(BF16)
        fc = D_FF // n_chunks
        acc = x_ref[...] if residual else None
        for c in range(n_chunks):
            g = _dot(h, wg_ref[0, :, c * fc:(c + 1) * fc])
            u = _dot(h, wu_ref[0, :, c * fc:(c + 1) * fc])
            a = (_silu(g) * u).astype(BF16)
            d = _dot(a, wd_ref[0, c * fc:(c + 1) * fc, :])
            acc = d if acc is None else acc + d
        o_ref[...] = acc

    @pl.when(i >= nu_ref[0])
    def _():
        o_ref[...] = jnp.zeros_like(o_ref)


def _swiglu(rows, tile_expert, n_used, w_gate, w_up, w_down, tm, x=None):
    r = rows.shape[0]
    residual = x is not None
    row = lambda i, te, nu: (i, 0)
    wspec = lambda shape: pl.BlockSpec(shape, lambda i, te, nu: (te[i], 0, 0),
                                       pipeline_mode=pl.Buffered(1))
    in_specs = [pl.BlockSpec((tm, D_MODEL), row)]
    args = [rows]
    if residual:
        in_specs.append(pl.BlockSpec((tm, D_MODEL), row))
        args.append(x)
    in_specs += [wspec((1, D_MODEL, D_FF)), wspec((1, D_MODEL, D_FF)), wspec((1, D_FF, D_MODEL))]
    args += [w_gate, w_up, w_down]
    grid_spec = pltpu.PrefetchScalarGridSpec(
        num_scalar_prefetch=2,
        grid=(r // tm,),
        in_specs=in_specs,
        out_specs=pl.BlockSpec((tm, D_MODEL), row),
    )
    return pl.pallas_call(
        functools.partial(_swiglu_kernel, residual, 2),
        grid_spec=grid_spec,
        out_shape=jax.ShapeDtypeStruct((r, D_MODEL), F32),
        compiler_params=_params("arbitrary"),
        name="swiglu_dense" if residual else "swiglu_experts",
    )(tile_expert, n_used, *args)


ROW_DMA_UNROLL = 8


def _dispatch_kernel(pos_ref, h_ref, buf_ref, o_ref, sem):
    del buf_ref
    tm = h_ref.shape[0]

    def row_copy(r, slot):
        dst = pos_ref[0, 0, 2 * r + slot]
        return pltpu.make_async_copy(h_ref.at[pl.ds(r, 1), :], o_ref.at[pl.ds(dst, 1), :], sem)

    def issue(r, carry):
        row_copy(r, 0).start()
        row_copy(r, 1).start()
        return carry

    def drain(r, carry):
        row_copy(r, 0).wait()
        row_copy(r, 1).wait()
        return carry

    lax.fori_loop(0, tm, issue, 0, unroll=ROW_DMA_UNROLL)
    lax.fori_loop(0, tm, drain, 0, unroll=ROW_DMA_UNROLL)


def _dispatch(hn, pos, r_pad, tm):
    t = hn.shape[0]
    pos3 = pos.reshape(t // tm, 1, 2 * tm)
    buf = jnp.zeros((r_pad, D_MODEL), F32)
    return pl.pallas_call(
        _dispatch_kernel,
        grid=(t // tm,),
        in_specs=[
            pl.BlockSpec((1, 1, 2 * tm), lambda i: (i, 0, 0), memory_space=pltpu.SMEM),
            pl.BlockSpec((tm, D_MODEL), lambda i: (i, 0)),
            pl.BlockSpec(memory_space=pl.ANY),
        ],
        out_specs=pl.BlockSpec(memory_space=pl.ANY),
        out_shape=jax.ShapeDtypeStruct((r_pad, D_MODEL), F32),
        scratch_shapes=[pltpu.SemaphoreType.DMA],
        input_output_aliases={2: 0},
        compiler_params=_params("arbitrary"),
        name="moe_dispatch",
    )(pos3, hn, buf)


def _combine_kernel(pos_ref, next_pos_ref, route_ref, x_ref, g_ref, y_ref, o_ref, buf, sem):
    tm = x_ref.shape[0]
    i = pl.program_id(0)
    par = i % 2

    def row_copy(idx_ref, r, slot, par):
        src = idx_ref[0, 0, 2 * r + slot]
        return pltpu.make_async_copy(y_ref.at[pl.ds(src, 1), :], buf.at[par, slot, pl.ds(r, 1), :],
                                     sem.at[par])

    def issue_all(idx_ref, par):
        def issue(r, carry):
            row_copy(idx_ref, r, 0, par).start()
            row_copy(idx_ref, r, 1, par).start()
            return carry
        lax.fori_loop(0, tm, issue, 0, unroll=ROW_DMA_UNROLL)

    def drain(r, carry):
        row_copy(pos_ref, r, 0, par).wait()
        row_copy(pos_ref, r, 1, par).wait()
        return carry

    @pl.when(i == 0)
    def _():
        issue_all(pos_ref, 0)

    @pl.when(i + 1 < pl.num_programs(0))
    def _():
        issue_all(next_pos_ref, 1 - par)

    lax.fori_loop(0, tm, drain, 0, unroll=ROW_DMA_UNROLL)
    route = route_ref[...]
    x = x_ref[...] + route[:, 2:3] * buf[par, 0] + route[:, 3:4] * buf[par, 1]
    o_ref[...] = _rms(x, g_ref[...])


def _combine(pos, route, x, final_norm, y_sorted, tm):
    t = x.shape[0]
    n = t // tm
    pos3 = pos.reshape(n, 1, 2 * tm)
    return pl.pallas_call(
        _combine_kernel,
        grid=(n,),
        in_specs=[
            pl.BlockSpec((1, 1, 2 * tm), lambda i: (i, 0, 0), memory_space=pltpu.SMEM),
            pl.BlockSpec((1, 1, 2 * tm), lambda i: (jnp.minimum(i + 1, n - 1), 0, 0),
                         memory_space=pltpu.SMEM),
            pl.BlockSpec((tm, LANES), lambda i: (i, 0)),
            pl.BlockSpec((tm, D_MODEL), lambda i: (i, 0)),
            pl.BlockSpec((1, D_MODEL), lambda i: (0, 0)),
            pl.BlockSpec(memory_space=pl.ANY),
        ],
        out_specs=pl.BlockSpec((tm, D_MODEL), lambda i: (i, 0)),
        out_shape=jax.ShapeDtypeStruct((t, D_MODEL), F32),
        scratch_shapes=[pltpu.VMEM((2, 2, tm, D_MODEL), F32), pltpu.SemaphoreType.DMA((2,))],
        compiler_params=_params("arbitrary"),
        name="moe_combine",
    )(pos3, pos3, route, x, final_norm.astype(F32)[None, :], y_sorted)


def _tile(n, want):
    t = min(n, want)
    assert n % t == 0
    return t


def _split_w_in(w):
    sizes = (D_SSD, CONV_CH, SSD_HEADS, D_FOX, D_FOX, D_FOX, FOX_HEADS)
    offs = [0]
    for s in sizes:
        offs.append(offs[-1] + s)
    z, xbc, dt, q, k, v, f = (w[:, offs[i]:offs[i + 1]] for i in range(7))
    w_main = jnp.concatenate([xbc, z, q, k, v], axis=1).astype(BF16)
    small = jnp.concatenate([dt, f], axis=1)
    w_small = jnp.zeros((D_MODEL, LANES), F32).at[:, :N_SMALL].set(small).astype(BF16)
    return w_main, w_small, small.T.astype(BF16)


def kernel(x, mix_norm, w_in, conv_w, conv_b, dt_bias, a_log, d_skip, ssd_norm, fox_f_bias, fox_norm,
           w_out, ffn_norm, ffn_w_gate, ffn_w_up, ffn_w_down, router_w, moe_w_gate, moe_w_up,
           moe_w_down, final_norm):
    bsz, seq, _ = x.shape
    t = bsz * seq
    depth = mix_norm.shape[0]
    assert depth == 2 and seq % CHUNK == 0
    x = x.reshape(t, D_MODEL).astype(F32)
    tm_proj = _tile(t, 1024)
    tm_out = _tile(t, 512)
    tm_ffn = _tile(t, 512)
    tm_moe = _tile(t, 512)
    tm_row = _tile(t, 256)
    tq = _tile(seq, 2048)

    def mixer(i, x, next_norm, router=None):
        w_main, w_small, w_small_t = _split_w_in(w_in[i])
        proj, small, small_t = _inproj(x, mix_norm[i].astype(F32)[None, :], w_main, w_small, w_small_t,
                                       tm_proj, 2048)
        y_ssd = _ssd(proj, small, small_t, conv_w[i], conv_b[i], dt_bias[i], a_log[i], d_skip[i],
                     bsz, seq)
        fx = _fprep(small, fox_f_bias[i], bsz, seq, _tile(seq, 1024))
        y_fox = _attention(proj, fx, bsz, seq, tq)
        return _outproj(y_ssd, proj, y_fox, x, ssd_norm[i], fox_norm[i], w_out[i], next_norm, tm_out,
                        router)

    x, hn = mixer(0, x, ffn_norm[0])
    n_tiles = t // tm_ffn
    x = _swiglu(hn, jnp.zeros((n_tiles,), jnp.int32), jnp.full((1,), n_tiles, jnp.int32),
                ffn_w_gate.astype(BF16), ffn_w_up.astype(BF16), ffn_w_down.astype(BF16), tm_ffn, x=x)

    x, hn, route, counts = mixer(1, x, ffn_norm[1], router_w[0])
    cnt = counts[0, :N_EXPERTS].astype(jnp.int32)
    group = ((cnt + tm_moe - 1) // tm_moe) * tm_moe
    ends = jnp.cumsum(group)
    starts = ends - group
    idx = route[:, 0:2].astype(jnp.int32)
    rank = route[:, 4:6].astype(jnp.int32)
    pos = (jnp.take(starts, idx) + rank).astype(jnp.int32)
    n_moe_tiles = (2 * t) // tm_moe + N_EXPERTS
    r_pad = n_moe_tiles * tm_moe
    tile_start = jnp.arange(n_moe_tiles, dtype=jnp.int32) * tm_moe
    tile_expert = jnp.minimum(jnp.sum(tile_start[:, None] >= ends[None, :], axis=1), N_EXPERTS - 1)
    n_used = (ends[-1:] // tm_moe).astype(jnp.int32)
    rows = _dispatch(hn, pos, r_pad, tm_row)
    y_sorted = _swiglu(rows, tile_expert.astype(jnp.int32), n_used, moe_w_gate[0].astype(BF16),
                       moe_w_up[0].astype(BF16), moe_w_down[0].astype(BF16), tm_moe)
    out = _combine(pos, route, x, final_norm, y_sorted, tm_row)
    return out.reshape(bsz, seq, D_MODEL)
```

```python
import functools
import math

import jax
import jax.numpy as jnp
from jax import lax
from jax.experimental import pallas as pl
from jax.experimental.pallas import tpu as pltpu

F32 = jnp.float32
BF16 = jnp.bfloat16

D_MODEL = 1024
D_SSD = 1024
SSD_HEAD_DIM = 64
SSD_HEADS = 16
SSD_GROUPS = 4
SSD_STATE = 128
CONV_WIDTH = 4
CHUNK = 128
CONV_CH = D_SSD + 2 * SSD_GROUPS * SSD_STATE
D_FOX = 1024
FOX_HEAD_DIM = 64
FOX_HEADS = 16
D_MIX = D_SSD + D_FOX
D_FF = 2816
N_EXPERTS = 8
EPS = 1e-5

LANES = 128
SUBLANES = 8
VMEM_LIMIT_BYTES = 56 * 1024 * 1024

D_MAIN = CONV_CH + D_SSD + 3 * D_FOX
COL_Z = CONV_CH
COL_Q = COL_Z + D_SSD
COL_K = COL_Q + D_FOX
COL_V = COL_K + D_FOX
N_SMALL = SSD_HEADS + FOX_HEADS


def _params(*sem):
    return pltpu.CompilerParams(dimension_semantics=sem, vmem_limit_bytes=VMEM_LIMIT_BYTES)


def _split3(v):
    hi = v.astype(BF16)
    r1 = v - hi.astype(F32)
    mid = r1.astype(BF16)
    lo = (r1 - mid.astype(F32)).astype(BF16)
    return hi, mid, lo


def _dot(a, b):
    return jnp.dot(a, b, preferred_element_type=F32)


def _dot_nt(a, b):
    return lax.dot_general(a, b, (((1,), (1,)), ((), ())), preferred_element_type=F32)


def _dot_sel_right(v, sel_bf16):
    hi, mid, lo = _split3(v)
    return _dot(hi, sel_bf16) + _dot(mid, sel_bf16) + _dot(lo, sel_bf16)


def _dot_sel_left(sel_bf16, v):
    hi, mid, lo = _split3(v)
    return _dot(sel_bf16, hi) + _dot(sel_bf16, mid) + _dot(sel_bf16, lo)


def _softplus(x):
    return jnp.maximum(x, 0.0) + jnp.log1p(jnp.exp(-jnp.abs(x)))


def _silu(x):
    return x * jax.nn.sigmoid(x)


def _rms(x, g):
    ms = jnp.mean(x * x, axis=-1, keepdims=True)
    return (x * lax.rsqrt(ms + EPS)) * g


def _inproj_kernel(x_ref, g_ref, w_ref, ws_ref, wst_ref, proj_ref, small_ref, smallt_ref, hn_scr):
    @pl.when(pl.program_id(1) == 0)
    def _():
        hb = _rms(x_ref[...], g_ref[...]).astype(BF16)
        hn_scr[...] = hb
        small_ref[...] = _dot(hb, ws_ref[...])
        smallt_ref[...] = _dot_nt(wst_ref[...], hb)

    proj_ref[...] = _dot(hn_scr[...], w_ref[...]).astype(BF16)


def _inproj(x, g, w_main, w_small, w_small_t, tm, tn):
    t = x.shape[0]
    return pl.pallas_call(
        _inproj_kernel,
        grid=(t // tm, D_MAIN // tn),
        in_specs=[
            pl.BlockSpec((tm, D_MODEL), lambda i, j: (i, 0)),
            pl.BlockSpec((1, D_MODEL), lambda i, j: (0, 0)),
            pl.BlockSpec((D_MODEL, tn), lambda i, j: (0, j)),
            pl.BlockSpec((D_MODEL, LANES), lambda i, j: (0, 0)),
            pl.BlockSpec((N_SMALL, D_MODEL), lambda i, j: (0, 0)),
        ],
        out_specs=[
            pl.BlockSpec((tm, tn), lambda i, j: (i, j)),
            pl.BlockSpec((tm, LANES), lambda i, j: (i, 0)),
            pl.BlockSpec((N_SMALL, tm), lambda i, j: (0, i)),
        ],
        out_shape=[
            jax.ShapeDtypeStruct((t, D_MAIN), BF16),
            jax.ShapeDtypeStruct((t, LANES), F32),
            jax.ShapeDtypeStruct((N_SMALL, t), F32),
        ],
        scratch_shapes=[pltpu.VMEM((tm, D_MODEL), BF16)],
        compiler_params=_params("parallel", "arbitrary"),
        name="inproj",
    )(x, g, w_main, w_small, w_small_t)


N_PIECES = 3


def _extra_lane(head):
    return (head // 2) * LANES + (FOX_HEAD_DIM if head % 2 == 0 else 0)


def _fprep_kernel(f_ref, fb_ref, tril_ref, place_ref, out_ref, carry):
    @pl.when(pl.program_id(1) == 0)
    def _():
        carry[...] = jnp.zeros_like(carry)

    tl = f_ref.shape[0]
    lane = lax.broadcasted_iota(jnp.int32, (tl, LANES), 1)
    is_f = (lane >= SSD_HEADS) & (lane < N_SMALL)
    logf = jnp.where(is_f, -_softplus(-(f_ref[...] + fb_ref[...])), 0.0)
    cum = _dot_sel_left(tril_ref[...], logf) + carry[0:1, :]
    carry[...] = jnp.broadcast_to(cum[tl - 1:tl, :], carry.shape)
    pieces = _split3(cum)
    out = _dot(pieces[0], place_ref[0])
    for i in range(1, N_PIECES):
        out = out + _dot(pieces[i], place_ref[i])
    out_ref[...] = out.astype(BF16)


def _fprep(small, f_bias, bsz, seq, tl):
    nl = seq // tl
    fb = jnp.zeros((1, LANES), F32).at[0, SSD_HEADS:N_SMALL].set(f_bias.astype(F32))
    r = lax.broadcasted_iota(jnp.int32, (tl, tl), 0)
    c = lax.broadcasted_iota(jnp.int32, (tl, tl), 1)
    tril = (c <= r).astype(BF16)
    heads = jnp.arange(FOX_HEADS)
    lanes = jnp.asarray([_extra_lane(h) for h in range(FOX_HEADS)])
    place = jnp.zeros((N_PIECES, LANES, D_FOX), F32)
    for i in range(N_PIECES):
        place = place.at[i, SSD_HEADS + heads, lanes + i].set(-1.0)
    return pl.pallas_call(
        _fprep_kernel,
        grid=(bsz, nl),
        in_specs=[
            pl.BlockSpec((tl, LANES), lambda b, j: (b * nl + j, 0)),
            pl.BlockSpec((1, LANES), lambda b, j: (0, 0)),
            pl.BlockSpec((tl, tl), lambda b, j: (0, 0)),
            pl.BlockSpec((N_PIECES, LANES, D_FOX), lambda b, j: (0, 0, 0)),
        ],
        out_specs=pl.BlockSpec((tl, D_FOX), lambda b, j: (b * nl + j, 0)),
        out_shape=jax.ShapeDtypeStruct((bsz * seq, D_FOX), BF16),
        scratch_shapes=[pltpu.VMEM((SUBLANES, LANES), F32)],
        compiler_params=_params("parallel", "arbitrary"),
        name="fprep",
    )(small, fb, tril, place.astype(BF16))


def _ssd_kernel(xbc_ref, dt_ref, dtt_ref, cw_ref, cb_ref, dtb_ref, alog_ref, dtbc_ref,
                alogc_ref, exp_ref, dsk_ref, tril_ref, triu_ref, out_ref, xtail, ytail, state):
    pad = SUBLANES
    assert CONV_WIDTH == 4

    @pl.when(pl.program_id(1) == 0)
    def _():
        xtail[...] = jnp.zeros_like(xtail)
        ytail[...] = jnp.zeros_like(ytail)
        state[...] = jnp.zeros_like(state)

    tile_row = lax.broadcasted_iota(jnp.int32, (pad, CONV_CH), 0)

    def delayed(cur, tail_ref, d):
        rolled = pltpu.roll(cur, d, 0)
        head = jnp.where(tile_row < d, pltpu.roll(tail_ref[...], d, 0), rolled[0:pad, :])
        return jnp.concatenate([head, rolled[pad:, :]], axis=0)

    x_now = xbc_ref[...].astype(F32)
    x_prev = delayed(x_now, xtail, 1)
    early = cw_ref[0:1, :] * x_prev + cw_ref[1:2, :] * x_now
    late = cw_ref[2:3, :] * x_prev + cw_ref[3:4, :] * x_now
    conv = (delayed(early, ytail, 2) + late) + cb_ref[...]
    xtail[...] = x_now[CHUNK - pad:, :]
    ytail[...] = early[CHUNK - pad:, :]
    u = _silu(conv)
    xs = u[:, :D_SSD]
    gs = SSD_GROUPS * SSD_STATE
    bm = u[:, D_SSD:D_SSD + gs]
    cm = u[:, D_SSD + gs:]

    expand = exp_ref[...]
    dt = _softplus(dt_ref[...] + dtb_ref[...])
    a = -jnp.exp(alog_ref[...])
    acs = _dot_sel_left(tril_ref[...], dt * a)
    dtx = _dot_sel_right(dt, expand)
    acsx = _dot_sel_right(acs, expand)
    acs_last_x = acsx[CHUNK - 1:CHUNK, :]
    dtt = _softplus(dtt_ref[...] + dtbc_ref[...])
    at = -jnp.exp(alogc_ref[...])
    acst = _dot_sel_right(dtt * at, triu_ref[...])

    xc = xs * dtx
    xc_b = xc.astype(BF16)
    xdec_b = (xc * jnp.exp(acs_last_x - acsx)).astype(BF16)
    prev_b = state[...].astype(BF16)

    row = lax.broadcasted_iota(jnp.int32, (CHUNK, CHUNK), 0)
    col = lax.broadcasted_iota(jnp.int32, (CHUNK, CHUNK), 1)
    causal = col <= row
    low_half = lax.broadcasted_iota(jnp.int32, (CHUNK, LANES), 1) < SSD_HEAD_DIM
    heads_per_group = SSD_HEADS // SSD_GROUPS
    gw = heads_per_group * SSD_HEAD_DIM

    y_diag = []
    y_off = []
    st_new = []
    for g in range(SSD_GROUPS):
        bg = bm[:, g * SSD_STATE:(g + 1) * SSD_STATE]
        cg = cm[:, g * SSD_STATE:(g + 1) * SSD_STATE].astype(BF16)
        cbg = _dot_nt(cg, bg.astype(BF16))
        yd = []
        for r in range(heads_per_group):
            h = g * heads_per_group + r
            seg = acs[:, h:h + 1] - acst[h:h + 1, :]
            dec = jnp.exp(jnp.where(causal, seg, -jnp.inf))
            m = (cbg * dec).astype(BF16)
            j = h // 2
            yd.append(_dot(m, xc_b[:, j * LANES:(j + 1) * LANES]))
        for r in range(0, heads_per_group, 2):
            y_diag.append(jnp.where(low_half, yd[r], yd[r + 1]))
        sl = slice(g * gw, (g + 1) * gw)
        st_new.append(_dot(bg.T.astype(BF16), xdec_b[:, sl]))
        y_off.append(_dot(cg, prev_b[:, sl]))
    y_diag = jnp.concatenate(y_diag, axis=1)
    y_off = jnp.concatenate(y_off, axis=1) * jnp.exp(acsx)
    state[...] = state[...] * jnp.exp(acs_last_x) + jnp.concatenate(st_new, axis=1)

    out_ref[...] = (y_diag + y_off + dsk_ref[...] * xs).astype(BF16)


def _ssd(proj, small, small_t, conv_w, conv_b, dt_bias, a_log, d_skip, bsz, seq):
    t = bsz * seq
    nc = seq // CHUNK
    pad_row = lambda v: jnp.zeros((1, LANES), F32).at[0, :SSD_HEADS].set(v.astype(F32))
    col = lambda v: jnp.broadcast_to(v.astype(F32)[:, None], (SSD_HEADS, CHUNK))
    hh = lax.broadcasted_iota(jnp.int32, (LANES, D_SSD), 0)
    cc = lax.broadcasted_iota(jnp.int32, (LANES, D_SSD), 1)
    expand = (cc // SSD_HEAD_DIM == hh).astype(BF16)
    r = lax.broadcasted_iota(jnp.int32, (CHUNK, CHUNK), 0)
    c = lax.broadcasted_iota(jnp.int32, (CHUNK, CHUNK), 1)
    tril = (c <= r).astype(BF16)
    triu = (r <= c).astype(BF16)
    dsk = jnp.repeat(d_skip.astype(F32), SSD_HEAD_DIM)[None, :]
    const = lambda shape: pl.BlockSpec(shape, lambda b, j: (0, 0))
    return pl.pallas_call(
        _ssd_kernel,
        grid=(bsz, nc),
        in_specs=[
            pl.BlockSpec((CHUNK, CONV_CH), lambda b, j: (b * nc + j, 0)),
            pl.BlockSpec((CHUNK, LANES), lambda b, j: (b * nc + j, 0)),
            pl.BlockSpec((SSD_HEADS, CHUNK), lambda b, j: (0, b * nc + j)),
            const((CONV_WIDTH, CONV_CH)),
            const((1, CONV_CH)),
            const((1, LANES)),
            const((1, LANES)),
            const((SSD_HEADS, CHUNK)),
            const((SSD_HEADS, CHUNK)),
            const((LANES, D_SSD)),
            const((1, D_SSD)),
            const((CHUNK, CHUNK)),
            const((CHUNK, CHUNK)),
        ],
        out_specs=pl.BlockSpec((CHUNK, D_SSD), lambda b, j: (b * nc + j, 0)),
        out_shape=jax.ShapeDtypeStruct((t, D_SSD), BF16),
        scratch_shapes=[
            pltpu.VMEM((SUBLANES, CONV_CH), F32),
            pltpu.VMEM((SUBLANES, CONV_CH), F32),
            pltpu.VMEM((SSD_STATE, D_SSD), F32),
        ],
        compiler_params=_params("parallel", "arbitrary"),
        name="ssd",
    )(proj, small, small_t, conv_w.astype(F32), conv_b.astype(F32)[None, :], pad_row(dt_bias),
      pad_row(a_log), col(dt_bias), col(a_log), expand, dsk, tril, triu)


ATT_SUB_Q = 512
ATT_MAX_Q = 1024
ATT_SUB_K = 512


def _attn_kernel(qi_tab, ki_tab, q_ref, k_ref, v_ref, fx_ref, o_ref, q_scr, m_scr, acc_scr):
    p = pl.program_id(2)
    qi = qi_tab[p]
    ki = ki_tab[p]
    tq = q_ref.shape[0]
    sub_q = min(ATT_SUB_Q, tq)
    lane_q = lax.broadcasted_iota(jnp.int32, (tq, LANES), 1)
    extra0 = (FOX_HEAD_DIM, 0)

    @pl.when(ki == 0)
    def _():
        q = q_ref[...].astype(F32) * (FOX_HEAD_DIM ** -0.5)
        for h in range(2):
            is_one = (lane_q >= extra0[h]) & (lane_q < extra0[h] + N_PIECES)
            mine = (lane_q < FOX_HEAD_DIM) if h == 0 else (lane_q >= FOX_HEAD_DIM)
            q_scr[h] = jnp.where(mine, q, jnp.where(is_one, 1.0, 0.0)).astype(BF16)
        m_scr[...] = jnp.full(m_scr.shape, -jnp.inf, F32)
        acc_scr[...] = jnp.zeros_like(acc_scr)

    def key_block(ks, diagonal, sub_k):
        k0 = ks * sub_k
        lane = lax.broadcasted_iota(jnp.int32, (sub_k, LANES), 1)
        low = lane < FOX_HEAD_DIM
        own = (low, jnp.logical_not(low))
        rk = pl.ds(k0, sub_k)
        k = k_ref[rk, :]
        v = v_ref[rk, :]
        fx = fx_ref[rk, :]
        for h in range(2):
            k_aug = jnp.where(own[h], k, fx)
            v_aug = jnp.where(own[h], v, jnp.where(lane == extra0[h], 1.0, 0.0).astype(BF16))
            q0 = 0
            while q0 < tq:
                if diagonal and k0 > q0 + sub_q - 1:
                    q0 += sub_q
                    continue
                rows = sub_q
                while (rows < min(ATT_MAX_Q, tq) and q0 % (2 * rows) == 0 and q0 + 2 * rows <= tq
                       and (not diagonal or k0 + sub_k - 1 <= q0)):
                    rows *= 2
                rq = pl.ds(q0, rows)
                s = _dot_nt(q_scr[h, rq, :], k_aug)
                if diagonal and k0 + sub_k - 1 > q0:
                    row = q0 + lax.broadcasted_iota(jnp.int32, (rows, sub_k), 0)
                    col = k0 + lax.broadcasted_iota(jnp.int32, (rows, sub_k), 1)
                    s = jnp.where(col <= row, s, -jnp.inf)
                q0 += rows
                tiles = [s[:, j * LANES:(j + 1) * LANES] for j in range(sub_k // LANES)]
                m_tile = functools.reduce(jnp.maximum, tiles)
                m_prev = m_scr[h, rq, :]
                m_new = jnp.maximum(m_prev, jnp.max(m_tile, axis=-1, keepdims=True))
                alpha = jnp.exp(m_prev - m_new)
                pr = jnp.concatenate([jnp.exp((t - m_new).astype(BF16)) for t in tiles], axis=1)
                m_scr[h, rq, :] = m_new
                acc_scr[h, rq, :] = alpha * acc_scr[h, rq, :] + _dot(pr, v_aug)

    @pl.when(ki < qi)
    def _():
        sub_k = min(ATT_SUB_K, tq)
        for ks in range(tq // sub_k):
            key_block(ks, False, sub_k)

    @pl.when(ki == qi)
    def _():
        sub_k = min(ATT_SUB_K, tq)
        for ks in range(tq // sub_k):
            key_block(ks, True, sub_k)
        outs = []
        for h in range(2):
            acc = acc_scr[h]
            outs.append(acc / acc[:, extra0[h]:extra0[h] + 1])
        o_ref[...] = jnp.where(lane_q < FOX_HEAD_DIM, outs[0], outs[1]).astype(BF16)


def _attention(proj, fx, bsz, seq, tq):
    t = bsz * seq
    nq = seq // tq
    pairs = [(q, k) for q in range(nq) for k in range(q + 1)]
    qi_tab = jnp.asarray([p[0] for p in pairs], jnp.int32)
    ki_tab = jnp.asarray([p[1] for p in pairs], jnp.int32)
    n_hp = FOX_HEADS // 2
    cq, ck, cv = COL_Q // LANES, COL_K // LANES, COL_V // LANES
    grid_spec = pltpu.PrefetchScalarGridSpec(
        num_scalar_prefetch=2,
        grid=(bsz, n_hp, len(pairs)),
        in_specs=[
            pl.BlockSpec((tq, LANES), lambda b, h, p, qt, kt: (b * nq + qt[p], cq + h)),
            pl.BlockSpec((tq, LANES), lambda b, h, p, qt, kt: (b * nq + kt[p], ck + h)),
            pl.BlockSpec((tq, LANES), lambda b, h, p, qt, kt: (b * nq + kt[p], cv + h)),
            pl.BlockSpec((tq, LANES), lambda b, h, p, qt, kt: (b * nq + kt[p], h)),
        ],
        out_specs=pl.BlockSpec((tq, LANES), lambda b, h, p, qt, kt: (b * nq + qt[p], h)),
        scratch_shapes=[
            pltpu.VMEM((2, tq, LANES), BF16),
            pltpu.VMEM((2, tq, LANES), F32),
            pltpu.VMEM((2, tq, LANES), F32),
        ],
    )
    return pl.pallas_call(
        _attn_kernel,
        grid_spec=grid_spec,
        out_shape=jax.ShapeDtypeStruct((t, D_FOX), BF16),
        compiler_params=_params("parallel", "parallel", "arbitrary"),
        name="fox_attention",
    )(qi_tab, ki_tab, proj, proj, proj, fx)


OUTPROJ_ROW_CHUNK = 256


def _outproj_kernel(moe, *refs):
    if moe:
        (ys_ref, z_ref, yf_ref, x_ref, sg_ref, fg_ref, w_ref, ng_ref, wr_ref, ltri_ref,
         xo_ref, hn_ref, route_ref, cnt_ref, carry) = refs
    else:
        ys_ref, z_ref, yf_ref, x_ref, sg_ref, fg_ref, w_ref, ng_ref, xo_ref, hn_ref = refs
    hn_chunks = []
    for r0 in range(0, x_ref.shape[0], OUTPROJ_ROW_CHUNK):
        rows = pl.ds(r0, OUTPROJ_ROW_CHUNK)
        gated = ys_ref[rows, :].astype(F32) * _silu(z_ref[rows, :].astype(F32))
        ys = _rms(gated, sg_ref[...]).astype(BF16)
        yf = _rms(yf_ref[rows, :].astype(F32), fg_ref[...]).astype(BF16)
        mix = _dot(ys, w_ref[0:D_SSD, :]) + _dot(yf, w_ref[D_SSD:D_MIX, :])
        x_chunk = x_ref[rows, :] + mix
        xo_ref[rows, :] = x_chunk
        hn_chunk = _rms(x_chunk, ng_ref[...])
        hn_ref[rows, :] = hn_chunk.astype(hn_ref.dtype)
        hn_chunks.append(hn_chunk)
    if not moe:
        return
    hn = jnp.concatenate(hn_chunks, axis=0)

    @pl.when(pl.program_id(0) == 0)
    def _():
        carry[...] = jnp.zeros_like(carry)

    tm = hn.shape[0]
    wr = wr_ref[...]
    h_hi = hn.astype(BF16)
    h_mid = (hn - h_hi.astype(F32)).astype(BF16)
    w_hi = wr.astype(BF16)
    w_mid = (wr - w_hi.astype(F32)).astype(BF16)
    logits = _dot(h_hi, w_hi) + _dot(h_hi, w_mid) + _dot(h_mid, w_hi)
    lane = lax.broadcasted_iota(jnp.int32, (tm, LANES), 1)
    lg = jnp.where(lane < N_EXPERTS, logits, -jnp.inf)
    m1 = jnp.max(lg, axis=-1, keepdims=True)
    i1 = jnp.min(jnp.where(lg == m1, lane, LANES), axis=-1, keepdims=True)
    lg2 = jnp.where(lane == i1, -jnp.inf, lg)
    m2 = jnp.max(lg2, axis=-1, keepdims=True)
    i2 = jnp.min(jnp.where(lg2 == m2, lane, LANES), axis=-1, keepdims=True)
    e2 = jnp.exp(m2 - m1)
    w1 = 1.0 / (1.0 + e2)
    w2 = e2 / (1.0 + e2)
    hit1 = lane == i1
    hit2 = lane == i2
    onehot = jnp.where(hit1 | hit2, 1.0, 0.0)
    before = _dot(ltri_ref[...], onehot.astype(BF16)) + carry[0:1, :]
    rank1 = jnp.sum(jnp.where(hit1, before, 0.0), axis=-1, keepdims=True)
    rank2 = jnp.sum(jnp.where(hit2, before, 0.0), axis=-1, keepdims=True)
    total = carry[0:1, :] + jnp.sum(onehot, axis=0, keepdims=True)
    carry[...] = jnp.broadcast_to(total, carry.shape)
    cnt_ref[...] = jnp.broadcast_to(total, cnt_ref.shape)
    route = jnp.where(lane == 0, i1.astype(F32), 0.0)
    route = jnp.where(lane == 1, i2.astype(F32), route)
    route = jnp.where(lane == 2, w1, route)
    route = jnp.where(lane == 3, w2, route)
    route = jnp.where(lane == 4, rank1, route)
    route = jnp.where(lane == 5, rank2, route)
    route_ref[...] = route


def _outproj(y_ssd, proj, y_fox, x, ssd_norm, fox_norm, w_out, next_norm, tm, router_w=None):
    t = x.shape[0]
    moe = router_w is not None
    row = lambda i: (i, 0)
    const = lambda i: (0, 0)
    in_specs = [
        pl.BlockSpec((tm, D_SSD), row),
        pl.BlockSpec((tm, D_SSD), lambda i: (i, COL_Z // D_SSD)),
        pl.BlockSpec((tm, D_FOX), row),
        pl.BlockSpec((tm, D_MODEL), row),
        pl.BlockSpec((1, D_SSD), const),
        pl.BlockSpec((1, D_FOX), const),
        pl.BlockSpec((D_MIX, D_MODEL), const),
        pl.BlockSpec((1, D_MODEL), const),
    ]
    args = [y_ssd, proj, y_fox, x, ssd_norm.astype(F32)[None, :], fox_norm.astype(F32)[None, :],
            w_out.astype(BF16), next_norm.astype(F32)[None, :]]
    out_specs = [pl.BlockSpec((tm, D_MODEL), row), pl.BlockSpec((tm, D_MODEL), row)]
    out_shape = [jax.ShapeDtypeStruct((t, D_MODEL), F32),
                 jax.ShapeDtypeStruct((t, D_MODEL), F32 if moe else BF16)]
    scratch = []
    if moe:
        wr = jnp.zeros((D_MODEL, LANES), F32).at[:, :N_EXPERTS].set(router_w.astype(F32))
        r = lax.broadcasted_iota(jnp.int32, (tm, tm), 0)
        c = lax.broadcasted_iota(jnp.int32, (tm, tm), 1)
        ltri = (c < r).astype(BF16)
        in_specs += [pl.BlockSpec((D_MODEL, LANES), const), pl.BlockSpec((tm, tm), const)]
        args += [wr, ltri]
        out_specs += [pl.BlockSpec((tm, LANES), row), pl.BlockSpec((SUBLANES, LANES), const)]
        out_shape += [jax.ShapeDtypeStruct((t, LANES), F32),
                      jax.ShapeDtypeStruct((SUBLANES, LANES), F32)]
        scratch = [pltpu.VMEM((SUBLANES, LANES), F32)]
    return pl.pallas_call(
        functools.partial(_outproj_kernel, moe),
        grid=(t // tm,),
        in_specs=in_specs,
        out_specs=out_specs,
        out_shape=out_shape,
        scratch_shapes=scratch,
        compiler_params=_params("arbitrary" if moe else "parallel"),
        name="outproj_moe" if moe else "outproj",
    )(*args)


def _swiglu_kernel(residual, n_chunks, te_ref, nu_ref, *refs):
    if residual:
        h_ref, x_ref, wg_ref, wu_ref, wd_ref, o_ref = refs
    else:
        h_ref, wg_ref, wu_ref, wd_ref, o_ref = refs
    i = pl.program_id(0)

    @pl.when(i < nu_ref[0])
    def _():
        h = h_ref[...].astype(BF16)
        fc = D_FF // n_chunks
        acc = x_ref[...] if residual else None
        for c in range(n_chunks):
            g = _dot(h, wg_ref[0, :, c * fc:(c + 1) * fc])
            u = _dot(h, wu_ref[0, :, c * fc:(c + 1) * fc])
            a = (_silu(g) * u).astype(BF16)
            d = _dot(a, wd_ref[0, c * fc:(c + 1) * fc, :])
            acc = d if acc is None else acc + d
        o_ref[...] = acc

    @pl.when(i >= nu_ref[0])
    def _():
        o_ref[...] = jnp.zeros_like(o_ref)


def _swiglu(rows, tile_expert, n_used, w_gate, w_up, w_down, tm, x=None):
    r = rows.shape[0]
    residual = x is not None
    row = lambda i, te, nu: (i, 0)
    wspec = lambda shape: pl.BlockSpec(shape, lambda i, te, nu: (te[i], 0, 0),
                                       pipeline_mode=pl.Buffered(1))
    in_specs = [pl.BlockSpec((tm, D_MODEL), row)]
    args = [rows]
    if residual:
        in_specs.append(pl.BlockSpec((tm, D_MODEL), row))
        args.append(x)
    in_specs += [wspec((1, D_MODEL, D_FF)), wspec((1, D_MODEL, D_FF)), wspec((1, D_FF, D_MODEL))]
    args += [w_gate, w_up, w_down]
    grid_spec = pltpu.PrefetchScalarGridSpec(
        num_scalar_prefetch=2,
        grid=(r // tm,),
        in_specs=in_specs,
        out_specs=pl.BlockSpec((tm, D_MODEL), row),
    )
    return pl.pallas_call(
        functools.partial(_swiglu_kernel, residual, 11),
        grid_spec=grid_spec,
        out_shape=jax.ShapeDtypeStruct((r, D_MODEL), F32),
        compiler_params=_params("arbitrary"),
        name="swiglu_dense" if residual else "swiglu_experts",
    )(tile_expert, n_used, *args)


ROW_DMA_UNROLL = 8


def _dispatch_kernel(pos_ref, h_ref, buf_ref, o_ref, sem):
    del buf_ref
    tm = h_ref.shape[0]

    def row_copy(r, slot):
        dst = pos_ref[0, 0, 2 * r + slot]
        return pltpu.make_async_copy(h_ref.at[pl.ds(r, 1), :], o_ref.at[pl.ds(dst, 1), :], sem)

    def issue(r, carry):
        row_copy(r, 0).start()
        row_copy(r, 1).start()
        return carry

    def drain(r, carry):
        row_copy(r, 0).wait()
        row_copy(r, 1).wait()
        return carry

    lax.fori_loop(0, tm, issue, 0, unroll=ROW_DMA_UNROLL)
    lax.fori_loop(0, tm, drain, 0, unroll=ROW_DMA_UNROLL)


def _dispatch(hn, pos, r_pad, tm):
    t = hn.shape[0]
    pos3 = pos.reshape(t // tm, 1, 2 * tm)
    buf = jnp.zeros((r_pad, D_MODEL), F32)
    return pl.pallas_call(
        _dispatch_kernel,
        grid=(t // tm,),
        in_specs=[
            pl.BlockSpec((1, 1, 2 * tm), lambda i: (i, 0, 0), memory_space=pltpu.SMEM),
            pl.BlockSpec((tm, D_MODEL), lambda i: (i, 0)),
            pl.BlockSpec(memory_space=pl.ANY),
        ],
        out_specs=pl.BlockSpec(memory_space=pl.ANY),
        out_shape=jax.ShapeDtypeStruct((r_pad, D_MODEL), F32),
        scratch_shapes=[pltpu.SemaphoreType.DMA],
        input_output_aliases={2: 0},
        compiler_params=_params("arbitrary"),
        name="moe_dispatch",
    )(pos3, hn, buf)


def _combine_kernel(pos_ref, next_pos_ref, route_ref, x_ref, g_ref, y_ref, o_ref, buf, sem):
    tm = x_ref.shape[0]
    i = pl.program_id(0)
    par = i % 2

    def row_copy(idx_ref, r, slot, par):
        src = idx_ref[0, 0, 2 * r + slot]
        return pltpu.make_async_copy(y_ref.at[pl.ds(src, 1), :], buf.at[par, slot, pl.ds(r, 1), :],
                                     sem.at[par])

    def issue_all(idx_ref, par):
        def issue(r, carry):
            row_copy(idx_ref, r, 0, par).start()
            row_copy(idx_ref, r, 1, par).start()
            return carry
        lax.fori_loop(0, tm, issue, 0, unroll=ROW_DMA_UNROLL)

    def drain(r, carry):
        row_copy(pos_ref, r, 0, par).wait()
        row_copy(pos_ref, r, 1, par).wait()
        return carry

    @pl.when(i == 0)
    def _():
        issue_all(pos_ref, 0)

    @pl.when(i + 1 < pl.num_programs(0))
    def _():
        issue_all(next_pos_ref, 1 - par)

    lax.fori_loop(0, tm, drain, 0, unroll=ROW_DMA_UNROLL)
    route = route_ref[...]
    x = x_ref[...] + route[:, 2:3] * buf[par, 0] + route[:, 3:4] * buf[par, 1]
    o_ref[...] = _rms(x, g_ref[...])


def _combine(pos, route, x, final_norm, y_sorted, tm):
    t = x.shape[0]
    n = t // tm
    pos3 = pos.reshape(n, 1, 2 * tm)
    return pl.pallas_call(
        _combine_kernel,
        grid=(n,),
        in_specs=[
            pl.BlockSpec((1, 1, 2 * tm), lambda i: (i, 0, 0), memory_space=pltpu.SMEM),
            pl.BlockSpec((1, 1, 2 * tm), lambda i: (jnp.minimum(i + 1, n - 1), 0, 0),
                         memory_space=pltpu.SMEM),
            pl.BlockSpec((tm, LANES), lambda i: (i, 0)),
            pl.BlockSpec((tm, D_MODEL), lambda i: (i, 0)),
            pl.BlockSpec((1, D_MODEL), lambda i: (0, 0)),
            pl.BlockSpec(memory_space=pl.ANY),
        ],
        out_specs=pl.BlockSpec((tm, D_MODEL), lambda i: (i, 0)),
        out_shape=jax.ShapeDtypeStruct((t, D_MODEL), F32),
        scratch_shapes=[pltpu.VMEM((2, 2, tm, D_MODEL), F32), pltpu.SemaphoreType.DMA((2,))],
        compiler_params=_params("arbitrary"),
        name="moe_combine",
    )(pos3, pos3, route, x, final_norm.astype(F32)[None, :], y_sorted)


def _tile(n, want):
    t = min(n, want)
    assert n % t == 0
    return t


def _split_w_in(w):
    sizes = (D_SSD, CONV_CH, SSD_HEADS, D_FOX, D_FOX, D_FOX, FOX_HEADS)
    offs = [0]
    for s in sizes:
        offs.append(offs[-1] + s)
    z, xbc, dt, q, k, v, f = (w[:, offs[i]:offs[i + 1]] for i in range(7))
    w_main = jnp.concatenate([xbc, z, q, k, v], axis=1).astype(BF16)
    small = jnp.concatenate([dt, f], axis=1)
    w_small = jnp.zeros((D_MODEL, LANES), F32).at[:, :N_SMALL].set(small).astype(BF16)
    return w_main, w_small, small.T.astype(BF16)


def kernel(x, mix_norm, w_in, conv_w, conv_b, dt_bias, a_log, d_skip, ssd_norm, fox_f_bias, fox_norm,
           w_out, ffn_norm, ffn_w_gate, ffn_w_up, ffn_w_down, router_w, moe_w_gate, moe_w_up,
           moe_w_down, final_norm):
    bsz, seq, _ = x.shape
    t = bsz * seq
    depth = mix_norm.shape[0]
    assert depth == 2 and seq % CHUNK == 0
    x = x.reshape(t, D_MODEL).astype(F32)
    tm_proj = _tile(t, 1024)
    tm_out = _tile(t, 512)
    tm_ffn = _tile(t, 1024)
    tm_moe = _tile(t, 512)
    tm_row = _tile(t, 256)
    tq = _tile(seq, 2048)

    def mixer(i, x, next_norm, router=None):
        w_main, w_small, w_small_t = _split_w_in(w_in[i])
        proj, small, small_t = _inproj(x, mix_norm[i].astype(F32)[None, :], w_main, w_small, w_small_t,
                                       tm_proj, 2048)
        y_ssd = _ssd(proj, small, small_t, conv_w[i], conv_b[i], dt_bias[i], a_log[i], d_skip[i],
                     bsz, seq)
        fx = _fprep(small, fox_f_bias[i], bsz, seq, _tile(seq, 512))
        y_fox = _attention(proj, fx, bsz, seq, tq)
        return _outproj(y_ssd, proj, y_fox, x, ssd_norm[i], fox_norm[i], w_out[i], next_norm, tm_out,
                        router)

    x, hn = mixer(0, x, ffn_norm[0])
    n_tiles = t // tm_ffn
    x = _swiglu(hn, jnp.zeros((n_tiles,), jnp.int32), jnp.full((1,), n_tiles, jnp.int32),
                ffn_w_gate.astype(BF16), ffn_w_up.astype(BF16), ffn_w_down.astype(BF16), tm_ffn, x=x)

    x, hn, route, counts = mixer(1, x, ffn_norm[1], router_w[0])
    cnt = counts[0, :N_EXPERTS].astype(jnp.int32)
    group = ((cnt + tm_moe - 1) // tm_moe) * tm_moe
    ends = jnp.cumsum(group)
    starts = ends - group
    idx = route[:, 0:2].astype(jnp.int32)
    rank = route[:, 4:6].astype(jnp.int32)
    pos = (jnp.take(starts, idx) + rank).astype(jnp.int32)
    n_moe_tiles = (2 * t) // tm_moe + N_EXPERTS
    r_pad = n_moe_tiles * tm_moe
    tile_start = jnp.arange(n_moe_tiles, dtype=jnp.int32) * tm_moe
    tile_expert = jnp.minimum(jnp.sum(tile_start[:, None] >= ends[None, :], axis=1), N_EXPERTS - 1)
    n_used = (ends[-1:] // tm_moe).astype(jnp.int32)
    rows = _dispatch(hn, pos, r_pad, tm_row)
    y_sorted = _swiglu(rows, tile_expert.astype(jnp.int32), n_used, moe_w_gate[0].astype(BF16),
                       moe_w_up[0].astype(BF16), moe_w_down[0].astype(BF16), tm_moe)
    out = _combine(pos, route, x, final_norm, y_sorted, tm_row)
    return out.reshape(bsz, seq, D_MODEL)
```

```python
import functools
import math

import jax
import jax.numpy as jnp
from jax import lax
from jax.experimental import pallas as pl
from jax.experimental.pallas import tpu as pltpu

F32 = jnp.float32
BF16 = jnp.bfloat16

D_MODEL = 1024
D_SSD = 1024
SSD_HEAD_DIM = 64
SSD_HEADS = 16
SSD_GROUPS = 4
SSD_STATE = 128
CONV_WIDTH = 4
CHUNK = 128
CONV_CH = D_SSD + 2 * SSD_GROUPS * SSD_STATE
D_FOX = 1024
FOX_HEAD_DIM = 64
FOX_HEADS = 16
D_MIX = D_SSD + D_FOX
D_FF = 2816
N_EXPERTS = 8
EPS = 1e-5

LANES = 128
SUBLANES = 8
VMEM_LIMIT_BYTES = 56 * 1024 * 1024

D_MAIN = CONV_CH + D_SSD + 3 * D_FOX
COL_Z = CONV_CH
COL_Q = COL_Z + D_SSD
COL_K = COL_Q + D_FOX
COL_V = COL_K + D_FOX
N_SMALL = SSD_HEADS + FOX_HEADS


def _params(*sem):
    return pltpu.CompilerParams(dimension_semantics=sem, vmem_limit_bytes=VMEM_LIMIT_BYTES)


def _split3(v):
    hi = v.astype(BF16)
    r1 = v - hi.astype(F32)
    mid = r1.astype(BF16)
    lo = (r1 - mid.astype(F32)).astype(BF16)
    return hi, mid, lo


def _dot(a, b):
    return jnp.dot(a, b, preferred_element_type=F32)


def _dot_nt(a, b):
    return lax.dot_general(a, b, (((1,), (1,)), ((), ())), preferred_element_type=F32)


def _dot_sel_right(v, sel_bf16):
    hi, mid, lo = _split3(v)
    return _dot(hi, sel_bf16) + _dot(mid, sel_bf16) + _dot(lo, sel_bf16)


def _dot_sel_left(sel_bf16, v):
    hi, mid, lo = _split3(v)
    return _dot(sel_bf16, hi) + _dot(sel_bf16, mid) + _dot(sel_bf16, lo)


def _softplus(x):
    return jnp.maximum(x, 0.0) + jnp.log1p(jnp.exp(-jnp.abs(x)))


def _silu(x):
    return x * jax.nn.sigmoid(x)


def _rms(x, g):
    ms = jnp.mean(x * x, axis=-1, keepdims=True)
    return (x * lax.rsqrt(ms + EPS)) * g


def _inproj_kernel(x_ref, g_ref, w_ref, ws_ref, wst_ref, proj_ref, small_ref, smallt_ref, hn_scr):
    @pl.when(pl.program_id(1) == 0)
    def _():
        hb = _rms(x_ref[...], g_ref[...]).astype(BF16)
        hn_scr[...] = hb
        small_ref[...] = _dot(hb, ws_ref[...])
        smallt_ref[...] = _dot_nt(wst_ref[...], hb)

    proj_ref[...] = _dot(hn_scr[...], w_ref[...]).astype(BF16)


def _inproj(x, g, w_main, w_small, w_small_t, tm, tn):
    t = x.shape[0]
    return pl.pallas_call(
        _inproj_kernel,
        grid=(t // tm, D_MAIN // tn),
        in_specs=[
            pl.BlockSpec((tm, D_MODEL), lambda i, j: (i, 0)),
            pl.BlockSpec((1, D_MODEL), lambda i, j: (0, 0)),
            pl.BlockSpec((D_MODEL, tn), lambda i, j: (0, j)),
            pl.BlockSpec((D_MODEL, LANES), lambda i, j: (0, 0)),
            pl.BlockSpec((N_SMALL, D_MODEL), lambda i, j: (0, 0)),
        ],
        out_specs=[
            pl.BlockSpec((tm, tn), lambda i, j: (i, j)),
            pl.BlockSpec((tm, LANES), lambda i, j: (i, 0)),
            pl.BlockSpec((N_SMALL, tm), lambda i, j: (0, i)),
        ],
        out_shape=[
            jax.ShapeDtypeStruct((t, D_MAIN), BF16),
            jax.ShapeDtypeStruct((t, LANES), F32),
            jax.ShapeDtypeStruct((N_SMALL, t), F32),
        ],
        scratch_shapes=[pltpu.VMEM((tm, D_MODEL), BF16)],
        compiler_params=_params("parallel", "arbitrary"),
        name="inproj",
    )(x, g, w_main, w_small, w_small_t)


N_PIECES = 3


def _extra_lane(head):
    return (head // 2) * LANES + (FOX_HEAD_DIM if head % 2 == 0 else 0)


def _fprep_kernel(f_ref, fb_ref, tril_ref, place_ref, out_ref, carry):
    @pl.when(pl.program_id(1) == 0)
    def _():
        carry[...] = jnp.zeros_like(carry)

    tl = f_ref.shape[0]
    lane = lax.broadcasted_iota(jnp.int32, (tl, LANES), 1)
    is_f = (lane >= SSD_HEADS) & (lane < N_SMALL)
    logf = jnp.where(is_f, -_softplus(-(f_ref[...] + fb_ref[...])), 0.0)
    cum = _dot_sel_left(tril_ref[...], logf) + carry[0:1, :]
    carry[...] = jnp.broadcast_to(cum[tl - 1:tl, :], carry.shape)
    pieces = _split3(cum)
    out = _dot(pieces[0], place_ref[0])
    for i in range(1, N_PIECES):
        out = out + _dot(pieces[i], place_ref[i])
    out_ref[...] = out.astype(BF16)


def _fprep(small, f_bias, bsz, seq, tl):
    nl = seq // tl
    fb = jnp.zeros((1, LANES), F32).at[0, SSD_HEADS:N_SMALL].set(f_bias.astype(F32))
    r = lax.broadcasted_iota(jnp.int32, (tl, tl), 0)
    c = lax.broadcasted_iota(jnp.int32, (tl, tl), 1)
    tril = (c <= r).astype(BF16)
    heads = jnp.arange(FOX_HEADS)
    lanes = jnp.asarray([_extra_lane(h) for h in range(FOX_HEADS)])
    place = jnp.zeros((N_PIECES, LANES, D_FOX), F32)
    for i in range(N_PIECES):
        place = place.at[i, SSD_HEADS + heads, lanes + i].set(-1.0)
    return pl.pallas_call(
        _fprep_kernel,
        grid=(bsz, nl),
        in_specs=[
            pl.BlockSpec((tl, LANES), lambda b, j: (b * nl + j, 0)),
            pl.BlockSpec((1, LANES), lambda b, j: (0, 0)),
            pl.BlockSpec((tl, tl), lambda b, j: (0, 0)),
            pl.BlockSpec((N_PIECES, LANES, D_FOX), lambda b, j: (0, 0, 0)),
        ],
        out_specs=pl.BlockSpec((tl, D_FOX), lambda b, j: (b * nl + j, 0)),
        out_shape=jax.ShapeDtypeStruct((bsz * seq, D_FOX), BF16),
        scratch_shapes=[pltpu.VMEM((SUBLANES, LANES), F32)],
        compiler_params=_params("parallel", "arbitrary"),
        name="fprep",
    )(small, fb, tril, place.astype(BF16))


def _ssd_kernel(xbc_ref, dt_ref, dtt_ref, cw_ref, cb_ref, dtb_ref, alog_ref, dtbc_ref,
                alogc_ref, exp_ref, dsk_ref, tril_ref, triu_ref, out_ref, xtail, ytail, state):
    pad = SUBLANES
    assert CONV_WIDTH == 4

    @pl.when(pl.program_id(1) == 0)
    def _():
        xtail[...] = jnp.zeros_like(xtail)
        ytail[...] = jnp.zeros_like(ytail)
        state[...] = jnp.zeros_like(state)

    tile_row = lax.broadcasted_iota(jnp.int32, (pad, CONV_CH), 0)

    def delayed(cur, tail_ref, d):
        rolled = pltpu.roll(cur, d, 0)
        head = jnp.where(tile_row < d, pltpu.roll(tail_ref[...], d, 0), rolled[0:pad, :])
        return jnp.concatenate([head, rolled[pad:, :]], axis=0)

    x_now = xbc_ref[...].astype(F32)
    x_prev = delayed(x_now, xtail, 1)
    early = cw_ref[0:1, :] * x_prev + cw_ref[1:2, :] * x_now
    late = cw_ref[2:3, :] * x_prev + cw_ref[3:4, :] * x_now
    conv = (delayed(early, ytail, 2) + late) + cb_ref[...]
    xtail[...] = x_now[CHUNK - pad:, :]
    ytail[...] = early[CHUNK - pad:, :]
    u = _silu(conv)
    xs = u[:, :D_SSD]
    gs = SSD_GROUPS * SSD_STATE
    bm = u[:, D_SSD:D_SSD + gs]
    cm = u[:, D_SSD + gs:]

    expand = exp_ref[...]
    dt = _softplus(dt_ref[...] + dtb_ref[...])
    a = -jnp.exp(alog_ref[...])
    acs = _dot_sel_left(tril_ref[...], dt * a)
    dtx = _dot_sel_right(dt, expand)
    acsx = _dot_sel_right(acs, expand)
    acs_last_x = acsx[CHUNK - 1:CHUNK, :]
    dtt = _softplus(dtt_ref[...] + dtbc_ref[...])
    at = -jnp.exp(alogc_ref[...])
    acst = _dot_sel_right(dtt * at, triu_ref[...])

    xc = xs * dtx
    xc_b = xc.astype(BF16)
    xdec_b = (xc * jnp.exp(acs_last_x - acsx)).astype(BF16)
    prev_b = state[...].astype(BF16)

    row = lax.broadcasted_iota(jnp.int32, (CHUNK, CHUNK), 0)
    col = lax.broadcasted_iota(jnp.int32, (CHUNK, CHUNK), 1)
    causal = col <= row
    low_half = lax.broadcasted_iota(jnp.int32, (CHUNK, LANES), 1) < SSD_HEAD_DIM
    heads_per_group = SSD_HEADS // SSD_GROUPS
    gw = heads_per_group * SSD_HEAD_DIM

    y_diag = []
    y_off = []
    st_new = []
    for g in range(SSD_GROUPS):
        bg = bm[:, g * SSD_STATE:(g + 1) * SSD_STATE]
        cg = cm[:, g * SSD_STATE:(g + 1) * SSD_STATE].astype(BF16)
        cbg = _dot_nt(cg, bg.astype(BF16))
        yd = []
        for r in range(heads_per_group):
            h = g * heads_per_group + r
            seg = acs[:, h:h + 1] - acst[h:h + 1, :]
            dec = jnp.exp(jnp.where(causal, seg, -jnp.inf))
            m = (cbg * dec).astype(BF16)
            j = h // 2
            yd.append(_dot(m, xc_b[:, j * LANES:(j + 1) * LANES]))
        for r in range(0, heads_per_group, 2):
            y_diag.append(jnp.where(low_half, yd[r], yd[r + 1]))
        sl = slice(g * gw, (g + 1) * gw)
        st_new.append(_dot(bg.T.astype(BF16), xdec_b[:, sl]))
        y_off.append(_dot(cg, prev_b[:, sl]))
    y_diag = jnp.concatenate(y_diag, axis=1)
    y_off = jnp.concatenate(y_off, axis=1) * jnp.exp(acsx)
    state[...] = state[...] * jnp.exp(acs_last_x) + jnp.concatenate(st_new, axis=1)

    out_ref[...] = (y_diag + y_off + dsk_ref[...] * xs).astype(BF16)


def _ssd(proj, small, small_t, conv_w, conv_b, dt_bias, a_log, d_skip, bsz, seq):
    t = bsz * seq
    nc = seq // CHUNK
    pad_row = lambda v: jnp.zeros((1, LANES), F32).at[0, :SSD_HEADS].set(v.astype(F32))
    col = lambda v: jnp.broadcast_to(v.astype(F32)[:, None], (SSD_HEADS, CHUNK))
    hh = lax.broadcasted_iota(jnp.int32, (LANES, D_SSD), 0)
    cc = lax.broadcasted_iota(jnp.int32, (LANES, D_SSD), 1)
    expand = (cc // SSD_HEAD_DIM == hh).astype(BF16)
    r = lax.broadcasted_iota(jnp.int32, (CHUNK, CHUNK), 0)
    c = lax.broadcasted_iota(jnp.int32, (CHUNK, CHUNK), 1)
    tril = (c <= r).astype(BF16)
    triu = (r <= c).astype(BF16)
    dsk = jnp.repeat(d_skip.astype(F32), SSD_HEAD_DIM)[None, :]
    const = lambda shape: pl.BlockSpec(shape, lambda b, j: (0, 0))
    return pl.pallas_call(
        _ssd_kernel,
        grid=(bsz, nc),
        in_specs=[
            pl.BlockSpec((CHUNK, CONV_CH), lambda b, j: (b * nc + j, 0)),
            pl.BlockSpec((CHUNK, LANES), lambda b, j: (b * nc + j, 0)),
            pl.BlockSpec((SSD_HEADS, CHUNK), lambda b, j: (0, b * nc + j)),
            const((CONV_WIDTH, CONV_CH)),
            const((1, CONV_CH)),
            const((1, LANES)),
            const((1, LANES)),
            const((SSD_HEADS, CHUNK)),
            const((SSD_HEADS, CHUNK)),
            const((LANES, D_SSD)),
            const((1, D_SSD)),
            const((CHUNK, CHUNK)),
            const((CHUNK, CHUNK)),
        ],
        out_specs=pl.BlockSpec((CHUNK, D_SSD), lambda b, j: (b * nc + j, 0)),
        out_shape=jax.ShapeDtypeStruct((t, D_SSD), BF16),
        scratch_shapes=[
            pltpu.VMEM((SUBLANES, CONV_CH), F32),
            pltpu.VMEM((SUBLANES, CONV_CH), F32),
            pltpu.VMEM((SSD_STATE, D_SSD), F32),
        ],
        compiler_params=_params("parallel", "arbitrary"),
        name="ssd",
    )(proj, small, small_t, conv_w.astype(F32), conv_b.astype(F32)[None, :], pad_row(dt_bias),
      pad_row(a_log), col(dt_bias), col(a_log), expand, dsk, tril, triu)


ATT_SUB_Q = 512
ATT_MAX_Q = 1024
ATT_SUB_K = 512


def _attn_kernel(qi_tab, ki_tab, q_ref, k_ref, v_ref, fx_ref, o_ref, q_scr, m_scr, acc_scr):
    p = pl.program_id(2)
    qi = qi_tab[p]
    ki = ki_tab[p]
    tq = q_ref.shape[0]
    sub_q = min(ATT_SUB_Q, tq)
    lane_q = lax.broadcasted_iota(jnp.int32, (tq, LANES), 1)
    extra0 = (FOX_HEAD_DIM, 0)

    @pl.when(ki == 0)
    def _():
        q = q_ref[...].astype(F32) * (FOX_HEAD_DIM ** -0.5)
        for h in range(2):
            is_one = (lane_q >= extra0[h]) & (lane_q < extra0[h] + N_PIECES)
            mine = (lane_q < FOX_HEAD_DIM) if h == 0 else (lane_q >= FOX_HEAD_DIM)
            q_scr[h] = jnp.where(mine, q, jnp.where(is_one, 1.0, 0.0)).astype(BF16)
        m_scr[...] = jnp.full(m_scr.shape, -jnp.inf, F32)
        acc_scr[...] = jnp.zeros_like(acc_scr)

    def key_block(ks, diagonal, sub_k):
        k0 = ks * sub_k
        lane = lax.broadcasted_iota(jnp.int32, (sub_k, LANES), 1)
        low = lane < FOX_HEAD_DIM
        own = (low, jnp.logical_not(low))
        rk = pl.ds(k0, sub_k)
        k = k_ref[rk, :]
        v = v_ref[rk, :]
        fx = fx_ref[rk, :]
        for h in range(2):
            k_aug = jnp.where(own[h], k, fx)
            v_aug = jnp.where(own[h], v, jnp.where(lane == extra0[h], 1.0, 0.0).astype(BF16))
            q0 = 0
            while q0 < tq:
                if diagonal and k0 > q0 + sub_q - 1:
                    q0 += sub_q
                    continue
                rows = sub_q
                while (rows < min(ATT_MAX_Q, tq) and q0 % (2 * rows) == 0 and q0 + 2 * rows <= tq
                       and (not diagonal or k0 + sub_k - 1 <= q0)):
                    rows *= 2
                rq = pl.ds(q0, rows)
                s = _dot_nt(q_scr[h, rq, :], k_aug)
                if diagonal and k0 + sub_k - 1 > q0:
                    row = q0 + lax.broadcasted_iota(jnp.int32, (rows, sub_k), 0)
                    col = k0 + lax.broadcasted_iota(jnp.int32, (rows, sub_k), 1)
                    s = jnp.where(col <= row, s, -jnp.inf)
                q0 += rows
                tiles = [s[:, j * LANES:(j + 1) * LANES] for j in range(sub_k // LANES)]
                m_tile = functools.reduce(jnp.maximum, tiles)
                m_prev = m_scr[h, rq, :]
                m_new = jnp.maximum(m_prev, jnp.max(m_tile, axis=-1, keepdims=True))
                alpha = jnp.exp(m_prev - m_new)
                pr = jnp.concatenate([jnp.exp((t - m_new).astype(BF16)) for t in tiles], axis=1)
                m_scr[h, rq, :] = m_new
                acc_scr[h, rq, :] = alpha * acc_scr[h, rq, :] + _dot(pr, v_aug)

    @pl.when(ki < qi)
    def _():
        sub_k = min(ATT_SUB_K, tq)
        for ks in range(tq // sub_k):
            key_block(ks, False, sub_k)

    @pl.when(ki == qi)
    def _():
        sub_k = min(ATT_SUB_K, tq)
        for ks in range(tq // sub_k):
            key_block(ks, True, sub_k)
        outs = []
        for h in range(2):
            acc = acc_scr[h]
            outs.append(acc / acc[:, extra0[h]:extra0[h] + 1])
        o_ref[...] = jnp.where(lane_q < FOX_HEAD_DIM, outs[0], outs[1]).astype(BF16)


def _attention(proj, fx, bsz, seq, tq):
    t = bsz * seq
    nq = seq // tq
    pairs = [(q, k) for q in range(nq) for k in range(q + 1)]
    qi_tab = jnp.asarray([p[0] for p in pairs], jnp.int32)
    ki_tab = jnp.asarray([p[1] for p in pairs], jnp.int32)
    n_hp = FOX_HEADS // 2
    cq, ck, cv = COL_Q // LANES, COL_K // LANES, COL_V // LANES
    grid_spec = pltpu.PrefetchScalarGridSpec(
        num_scalar_prefetch=2,
        grid=(bsz, n_hp, len(pairs)),
        in_specs=[
            pl.BlockSpec((tq, LANES), lambda b, h, p, qt, kt: (b * nq + qt[p], cq + h)),
            pl.BlockSpec((tq, LANES), lambda b, h, p, qt, kt: (b * nq + kt[p], ck + h)),
            pl.BlockSpec((tq, LANES), lambda b, h, p, qt, kt: (b * nq + kt[p], cv + h)),
            pl.BlockSpec((tq, LANES), lambda b, h, p, qt, kt: (b * nq + kt[p], h)),
        ],
        out_specs=pl.BlockSpec((tq, LANES), lambda b, h, p, qt, kt: (b * nq + qt[p], h)),
        scratch_shapes=[
            pltpu.VMEM((2, tq, LANES), BF16),
            pltpu.VMEM((2, tq, LANES), F32),
            pltpu.VMEM((2, tq, LANES), F32),
        ],
    )
    return pl.pallas_call(
        _attn_kernel,
        grid_spec=grid_spec,
        out_shape=jax.ShapeDtypeStruct((t, D_FOX), BF16),
        compiler_params=_params("parallel", "parallel", "arbitrary"),
        name="fox_attention",
    )(qi_tab, ki_tab, proj, proj, proj, fx)


OUTPROJ_ROW_CHUNK = 256


def _outproj_kernel(moe, *refs):
    if moe:
        (ys_ref, z_ref, yf_ref, x_ref, sg_ref, fg_ref, w_ref, ng_ref, wr_ref, ltri_ref,
         xo_ref, hn_ref, route_ref, cnt_ref, carry) = refs
    else:
        ys_ref, z_ref, yf_ref, x_ref, sg_ref, fg_ref, w_ref, ng_ref, xo_ref, hn_ref = refs
    hn_chunks = []
    for r0 in range(0, x_ref.shape[0], OUTPROJ_ROW_CHUNK):
        rows = pl.ds(r0, OUTPROJ_ROW_CHUNK)
        gated = ys_ref[rows, :].astype(F32) * _silu(z_ref[rows, :].astype(F32))
        ys = _rms(gated, sg_ref[...]).astype(BF16)
        yf = _rms(yf_ref[rows, :].astype(F32), fg_ref[...]).astype(BF16)
        mix = _dot(ys, w_ref[0:D_SSD, :]) + _dot(yf, w_ref[D_SSD:D_MIX, :])
        x_chunk = x_ref[rows, :] + mix
        xo_ref[rows, :] = x_chunk
        hn_chunk = _rms(x_chunk, ng_ref[...])
        hn_ref[rows, :] = hn_chunk.astype(hn_ref.dtype)
        hn_chunks.append(hn_chunk)
    if not moe:
        return
    hn = jnp.concatenate(hn_chunks, axis=0)

    @pl.when(pl.program_id(0) == 0)
    def _():
        carry[...] = jnp.zeros_like(carry)

    tm = hn.shape[0]
    wr = wr_ref[...]
    h_hi = hn.astype(BF16)
    h_mid = (hn - h_hi.astype(F32)).astype(BF16)
    w_hi = wr.astype(BF16)
    w_mid = (wr - w_hi.astype(F32)).astype(BF16)
    logits = _dot(h_hi, w_hi) + _dot(h_hi, w_mid) + _dot(h_mid, w_hi)
    lane = lax.broadcasted_iota(jnp.int32, (tm, LANES), 1)
    lg = jnp.where(lane < N_EXPERTS, logits, -jnp.inf)
    m1 = jnp.max(lg, axis=-1, keepdims=True)
    i1 = jnp.min(jnp.where(lg == m1, lane, LANES), axis=-1, keepdims=True)
    lg2 = jnp.where(lane == i1, -jnp.inf, lg)
    m2 = jnp.max(lg2, axis=-1, keepdims=True)
    i2 = jnp.min(jnp.where(lg2 == m2, lane, LANES), axis=-1, keepdims=True)
    e2 = jnp.exp(m2 - m1)
    w1 = 1.0 / (1.0 + e2)
    w2 = e2 / (1.0 + e2)
    hit1 = lane == i1
    hit2 = lane == i2
    onehot = jnp.where(hit1 | hit2, 1.0, 0.0)
    before = _dot(ltri_ref[...], onehot.astype(BF16)) + carry[0:1, :]
    rank1 = jnp.sum(jnp.where(hit1, before, 0.0), axis=-1, keepdims=True)
    rank2 = jnp.sum(jnp.where(hit2, before, 0.0), axis=-1, keepdims=True)
    total = carry[0:1, :] + jnp.sum(onehot, axis=0, keepdims=True)
    carry[...] = jnp.broadcast_to(total, carry.shape)
    cnt_ref[...] = jnp.broadcast_to(total, cnt_ref.shape)
    route = jnp.where(lane == 0, i1.astype(F32), 0.0)
    route = jnp.where(lane == 1, i2.astype(F32), route)
    route = jnp.where(lane == 2, w1, route)
    route = jnp.where(lane == 3, w2, route)
    route = jnp.where(lane == 4, rank1, route)
    route = jnp.where(lane == 5, rank2, route)
    route_ref[...] = route


def _outproj(y_ssd, proj, y_fox, x, ssd_norm, fox_norm, w_out, next_norm, tm, router_w=None):
    t = x.shape[0]
    moe = router_w is not None
    row = lambda i: (i, 0)
    const = lambda i: (0, 0)
    in_specs = [
        pl.BlockSpec((tm, D_SSD), row),
        pl.BlockSpec((tm, D_SSD), lambda i: (i, COL_Z // D_SSD)),
        pl.BlockSpec((tm, D_FOX), row),
        pl.BlockSpec((tm, D_MODEL), row),
        pl.BlockSpec((1, D_SSD), const),
        pl.BlockSpec((1, D_FOX), const),
        pl.BlockSpec((D_MIX, D_MODEL), const),
        pl.BlockSpec((1, D_MODEL), const),
    ]
    args = [y_ssd, proj, y_fox, x, ssd_norm.astype(F32)[None, :], fox_norm.astype(F32)[None, :],
            w_out.astype(BF16), next_norm.astype(F32)[None, :]]
    out_specs = [pl.BlockSpec((tm, D_MODEL), row), pl.BlockSpec((tm, D_MODEL), row)]
    out_shape = [jax.ShapeDtypeStruct((t, D_MODEL), F32),
                 jax.ShapeDtypeStruct((t, D_MODEL), F32 if moe else BF16)]
    scratch = []
    if moe:
        wr = jnp.zeros((D_MODEL, LANES), F32).at[:, :N_EXPERTS].set(router_w.astype(F32))
        r = lax.broadcasted_iota(jnp.int32, (tm, tm), 0)
        c = lax.broadcasted_iota(jnp.int32, (tm, tm), 1)
        ltri = (c < r).astype(BF16)
        in_specs += [pl.BlockSpec((D_MODEL, LANES), const), pl.BlockSpec((tm, tm), const)]
        args += [wr, ltri]
        out_specs += [pl.BlockSpec((tm, LANES), row), pl.BlockSpec((SUBLANES, LANES), const)]
        out_shape += [jax.ShapeDtypeStruct((t, LANES), F32),
                      jax.ShapeDtypeStruct((SUBLANES, LANES), F32)]
        scratch = [pltpu.VMEM((SUBLANES, LANES), F32)]
    return pl.pallas_call(
        functools.partial(_outproj_kernel, moe),
        grid=(t // tm,),
        in_specs=in_specs,
        out_specs=out_specs,
        out_shape=out_shape,
        scratch_shapes=scratch,
        compiler_params=_params("arbitrary" if moe else "parallel"),
        name="outproj_moe" if moe else "outproj",
    )(*args)


def _swiglu_kernel(residual, n_chunks, te_ref, nu_ref, *refs):
    if residual:
        h_ref, x_ref, wg_ref, wu_ref, wd_ref, o_ref = refs
    else:
        h_ref, wg_ref, wu_ref, wd_ref, o_ref = refs
    i = pl.program_id(0)

    @pl.when(i < nu_ref[0])
    def _():
        h = h_ref[...].astype(BF16)
        fc = D_FF // n_chunks
        acc = x_ref[...] if residual else None
        for c in range(n_chunks):
            g = _dot(h, wg_ref[0, :, c * fc:(c + 1) * fc])
            u = _dot(h, wu_ref[0, :, c * fc:(c + 1) * fc])
            a = (_silu(g) * u).astype(BF16)
            d = _dot(a, wd_ref[0, c * fc:(c + 1) * fc, :])
            acc = d if acc is None else acc + d
        o_ref[...] = acc

    @pl.when(i >= nu_ref[0])
    def _():
        o_ref[...] = jnp.zeros_like(o_ref)


def _swiglu(rows, tile_expert, n_used, w_gate, w_up, w_down, tm, x=None):
    r = rows.shape[0]
    residual = x is not None
    row = lambda i, te, nu: (i, 0)
    wspec = lambda shape: pl.BlockSpec(shape, lambda i, te, nu: (te[i], 0, 0),
                                       pipeline_mode=pl.Buffered(1))
    in_specs = [pl.BlockSpec((tm, D_MODEL), row)]
    args = [rows]
    if residual:
        in_specs.append(pl.BlockSpec((tm, D_MODEL), row))
        args.append(x)
    in_specs += [wspec((1, D_MODEL, D_FF)), wspec((1, D_MODEL, D_FF)), wspec((1, D_FF, D_MODEL))]
    args += [w_gate, w_up, w_down]
    grid_spec = pltpu.PrefetchScalarGridSpec(
        num_scalar_prefetch=2,
        grid=(r // tm,),
        in_specs=in_specs,
        out_specs=pl.BlockSpec((tm, D_MODEL), row),
    )
    return pl.pallas_call(
        functools.partial(_swiglu_kernel, residual, 11),
        grid_spec=grid_spec,
        out_shape=jax.ShapeDtypeStruct((r, D_MODEL), F32),
        compiler_params=_params("arbitrary"),
        name="swiglu_dense" if residual else "swiglu_experts",
    )(tile_expert, n_used, *args)


ROW_DMA_UNROLL = 8


def _dispatch_kernel(pos_ref, h_ref, buf_ref, o_ref, sem):
    del buf_ref
    tm = h_ref.shape[0]

    def row_copy(r, slot):
        dst = pos_ref[0, 0, 2 * r + slot]
        return pltpu.make_async_copy(h_ref.at[pl.ds(r, 1), :], o_ref.at[pl.ds(dst, 1), :], sem)

    def issue(r, carry):
        row_copy(r, 0).start(priority=0)
        row_copy(r, 1).start(priority=1)
        return carry

    def drain(r, carry):
        row_copy(r, 0).wait()
        row_copy(r, 1).wait()
        return carry

    lax.fori_loop(0, tm, issue, 0, unroll=ROW_DMA_UNROLL)
    lax.fori_loop(0, tm, drain, 0, unroll=ROW_DMA_UNROLL)


def _dispatch(hn, pos, r_pad, tm):
    t = hn.shape[0]
    pos3 = pos.reshape(t // tm, 1, 2 * tm)
    buf = jnp.zeros((r_pad, D_MODEL), F32)
    return pl.pallas_call(
        _dispatch_kernel,
        grid=(t // tm,),
        in_specs=[
            pl.BlockSpec((1, 1, 2 * tm), lambda i: (i, 0, 0), memory_space=pltpu.SMEM),
            pl.BlockSpec((tm, D_MODEL), lambda i: (i, 0)),
            pl.BlockSpec(memory_space=pl.ANY),
        ],
        out_specs=pl.BlockSpec(memory_space=pl.ANY),
        out_shape=jax.ShapeDtypeStruct((r_pad, D_MODEL), F32),
        scratch_shapes=[pltpu.SemaphoreType.DMA],
        input_output_aliases={2: 0},
        compiler_params=_params("arbitrary"),
        name="moe_dispatch",
    )(pos3, hn, buf)


def _combine_kernel(pos_ref, next_pos_ref, route_ref, x_ref, g_ref, y_ref, o_ref, buf, sem):
    tm = x_ref.shape[0]
    i = pl.program_id(0)
    par = i % 2

    def row_copy(idx_ref, r, slot, par):
        src = idx_ref[0, 0, 2 * r + slot]
        return pltpu.make_async_copy(y_ref.at[pl.ds(src, 1), :], buf.at[par, slot, pl.ds(r, 1), :],
                                     sem.at[par])

    def issue_all(idx_ref, par):
        def issue(r, carry):
            row_copy(idx_ref, r, 0, par).start(priority=0)
            row_copy(idx_ref, r, 1, par).start(priority=1)
            return carry
        lax.fori_loop(0, tm, issue, 0, unroll=ROW_DMA_UNROLL)

    def drain(r, carry):
        row_copy(pos_ref, r, 0, par).wait()
        row_copy(pos_ref, r, 1, par).wait()
        return carry

    @pl.when(i == 0)
    def _():
        issue_all(pos_ref, 0)

    @pl.when(i + 1 < pl.num_programs(0))
    def _():
        issue_all(next_pos_ref, 1 - par)

    lax.fori_loop(0, tm, drain, 0, unroll=ROW_DMA_UNROLL)
    route = route_ref[...]
    x = x_ref[...] + route[:, 2:3] * buf[par, 0] + route[:, 3:4] * buf[par, 1]
    o_ref[...] = _rms(x, g_ref[...])


def _combine(pos, route, x, final_norm, y_sorted, tm):
    t = x.shape[0]
    n = t // tm
    pos3 = pos.reshape(n, 1, 2 * tm)
    return pl.pallas_call(
        _combine_kernel,
        grid=(n,),
        in_specs=[
            pl.BlockSpec((1, 1, 2 * tm), lambda i: (i, 0, 0), memory_space=pltpu.SMEM),
            pl.BlockSpec((1, 1, 2 * tm), lambda i: (jnp.minimum(i + 1, n - 1), 0, 0),
                         memory_space=pltpu.SMEM),
            pl.BlockSpec((tm, LANES), lambda i: (i, 0)),
            pl.BlockSpec((tm, D_MODEL), lambda i: (i, 0)),
            pl.BlockSpec((1, D_MODEL), lambda i: (0, 0)),
            pl.BlockSpec(memory_space=pl.ANY),
        ],
        out_specs=pl.BlockSpec((tm, D_MODEL), lambda i: (i, 0)),
        out_shape=jax.ShapeDtypeStruct((t, D_MODEL), F32),
        scratch_shapes=[pltpu.VMEM((2, 2, tm, D_MODEL), F32), pltpu.SemaphoreType.DMA((2,))],
        compiler_params=_params("arbitrary"),
        name="moe_combine",
    )(pos3, pos3, route, x, final_norm.astype(F32)[None, :], y_sorted)


def _tile(n, want):
    t = min(n, want)
    assert n % t == 0
    return t


def _split_w_in(w):
    sizes = (D_SSD, CONV_CH, SSD_HEADS, D_FOX, D_FOX, D_FOX, FOX_HEADS)
    offs = [0]
    for s in sizes:
        offs.append(offs[-1] + s)
    z, xbc, dt, q, k, v, f = (w[:, offs[i]:offs[i + 1]] for i in range(7))
    w_main = jnp.concatenate([xbc, z, q, k, v], axis=1).astype(BF16)
    small = jnp.concatenate([dt, f], axis=1)
    w_small = jnp.zeros((D_MODEL, LANES), F32).at[:, :N_SMALL].set(small).astype(BF16)
    return w_main, w_small, small.T.astype(BF16)


def kernel(x, mix_norm, w_in, conv_w, conv_b, dt_bias, a_log, d_skip, ssd_norm, fox_f_bias, fox_norm,
           w_out, ffn_norm, ffn_w_gate, ffn_w_up, ffn_w_down, router_w, moe_w_gate, moe_w_up,
           moe_w_down, final_norm):
    bsz, seq, _ = x.shape
    t = bsz * seq
    depth = mix_norm.shape[0]
    assert depth == 2 and seq % CHUNK == 0
    x = x.reshape(t, D_MODEL).astype(F32)
    tm_proj = _tile(t, 1024)
    tm_out = _tile(t, 512)
    tm_ffn = _tile(t, 1024)
    tm_moe = _tile(t, 512)
    tm_row = _tile(t, 256)
    tq = _tile(seq, 2048)

    def mixer(i, x, next_norm, router=None):
        w_main, w_small, w_small_t = _split_w_in(w_in[i])
        proj, small, small_t = _inproj(x, mix_norm[i].astype(F32)[None, :], w_main, w_small, w_small_t,
                                       tm_proj, 2048)
        y_ssd = _ssd(proj, small, small_t, conv_w[i], conv_b[i], dt_bias[i], a_log[i], d_skip[i],
                     bsz, seq)
        fx = _fprep(small, fox_f_bias[i], bsz, seq, _tile(seq, 512))
        y_fox = _attention(proj, fx, bsz, seq, tq)
        return _outproj(y_ssd, proj, y_fox, x, ssd_norm[i], fox_norm[i], w_out[i], next_norm, tm_out,
                        router)

    x, hn = mixer(0, x, ffn_norm[0])
    n_tiles = t // tm_ffn
    x = _swiglu(hn, jnp.zeros((n_tiles,), jnp.int32), jnp.full((1,), n_tiles, jnp.int32),
                ffn_w_gate.astype(BF16), ffn_w_up.astype(BF16), ffn_w_down.astype(BF16), tm_ffn, x=x)

    x, hn, route, counts = mixer(1, x, ffn_norm[1], router_w[0])
    cnt = counts[0, :N_EXPERTS].astype(jnp.int32)
    group = ((cnt + tm_moe - 1) // tm_moe) * tm_moe
    ends = jnp.cumsum(group)
    starts = ends - group
    idx = route[:, 0:2].astype(jnp.int32)
    rank = route[:, 4:6].astype(jnp.int32)
    pos = (jnp.take(starts, idx) + rank).astype(jnp.int32)
    n_moe_tiles = (2 * t) // tm_moe + N_EXPERTS
    r_pad = n_moe_tiles * tm_moe
    tile_start = jnp.arange(n_moe_tiles, dtype=jnp.int32) * tm_moe
    tile_expert = jnp.minimum(jnp.sum(tile_start[:, None] >= ends[None, :], axis=1), N_EXPERTS - 1)
    n_used = (ends[-1:] // tm_moe).astype(jnp.int32)
    rows = _dispatch(hn, pos, r_pad, tm_row)
    y_sorted = _swiglu(rows, tile_expert.astype(jnp.int32), n_used, moe_w_gate[0].astype(BF16),
                       moe_w_up[0].astype(BF16), moe_w_down[0].astype(BF16), tm_moe)
    out = _combine(pos, route, x, final_norm, y_sorted, tm_row)
    return out.reshape(bsz, seq, D_MODEL)
```
